```python
import math
import functools
import jax
import jax.numpy as jnp
from jax import lax
import numpy as np

D_MODEL = 1024
BATCH = 2
SEQ = 8192
DEPTH = 2
DEC_BATCH = 128
DEC_SEQ = 8
PAST_LEN = 8192
PAGE_SIZE = 128

MLA_HEADS = 8
QK_NOPE = 64
QK_ROPE = 32
V_HEAD = 64
Q_LORA = 256
KV_LORA = 128
RET_HEADS = 4
RET_DK = 64
RET_DV = 64
HG_HEADS = 4
HG_DK = 64
HG_DV = 64
MIX_WIDTH = MLA_HEADS * V_HEAD + RET_HEADS * RET_DV + HG_HEADS * HG_DV
PEER_HEADS = 8
PEER_NKEYS = 128
PEER_EXPERTS = PEER_NKEYS * PEER_NKEYS
PEER_QDIM = 256
PEER_HALF = PEER_QDIM // 2
PEER_TOPK = 16
TOK_BLOCK = 256
Q_BLOCK = 128
CHUNK = 128
ROPE_BASE = 10000.0
EPS = 1e-6
IN_SIZES = (Q_LORA, KV_LORA, QK_ROPE,
            RET_HEADS * RET_DK, RET_HEADS * RET_DK, RET_HEADS * RET_DV, RET_HEADS * RET_DV,
            HG_HEADS * HG_DK, HG_HEADS * HG_DK, HG_HEADS * HG_DV, HG_HEADS * HG_DV)
IN_COLS = sum(IN_SIZES)

kernel_name = "hymba_mla_retnet_hgrn2_peer_step"


def rmsnorm(x, g):
    xf = x.astype(jnp.float32)
    y = xf * lax.rsqrt(jnp.mean(xf * xf, axis=-1, keepdims=True) + EPS)
    return (y * g.astype(jnp.float32)).astype(x.dtype)


def head_norm(x, g, center):
    B, T, H, d = x.shape
    xf = x.astype(jnp.float32)
    if center:
        xf = xf - jnp.mean(xf, axis=-1, keepdims=True)
    y = xf * lax.rsqrt(jnp.mean(xf * xf, axis=-1, keepdims=True) + EPS)
    return (y.reshape(B, T, H * d) * g.astype(jnp.float32)).astype(x.dtype)


def rope(x, pos):
    half = x.shape[-1] // 2
    inv = ROPE_BASE ** (-jnp.arange(half, dtype=jnp.float32) / half)
    ang = pos.astype(jnp.float32)[:, None] * inv[None, :]
    cos = jnp.cos(ang)[:, None, :].astype(x.dtype)
    sin = jnp.sin(ang)[:, None, :].astype(x.dtype)
    x1, x2 = x[..., :half], x[..., half:]
    return jnp.concatenate([x1 * cos - x2 * sin, x1 * sin + x2 * cos], axis=-1)


def split_cols(z):
    outs, o = [], 0
    for s in IN_SIZES:
        outs.append(z[..., o:o + s])
        o += s
    return outs


def to_chunks(a, C):
    B, T = a.shape[:2]
    return jnp.moveaxis(a.reshape((B, T // C, C) + a.shape[2:]), 1, 0)


def from_chunks(a):
    n, B, C = a.shape[:3]
    return jnp.moveaxis(a, 0, 1).reshape((B, n * C) + a.shape[3:])


def mla_prompt_attend(q_nope, q_rope, c_kv, k_rope, w_uk, w_uv):
    B, T, H, _ = q_nope.shape
    k_nope = jnp.einsum('btc,chn->bthn', c_kv, w_uk)
    v = jnp.einsum('btc,chv->bthv', c_kv, w_uv)
    nb = T // Q_BLOCK
    kpos = jnp.arange(T)
    scale = (QK_NOPE + QK_ROPE) ** -0.5

    def to_blocks(a):
        return jnp.moveaxis(a.reshape((B, nb, Q_BLOCK) + a.shape[2:]), 1, 0)

    def block(args):
        qn, qr, start = args
        s = jnp.einsum('bqhn,bkhn->bhqk', qn, k_nope) + jnp.einsum('bqhr,bkr->bhqk', qr, k_rope)
        s = s.astype(jnp.float32) * scale
        qpos = start + jnp.arange(Q_BLOCK)
        s = jnp.where(kpos[None, :] <= qpos[:, None], s, -jnp.inf)
        p = jax.nn.softmax(s, axis=-1).astype(v.dtype)
        return jnp.einsum('bhqk,bkhv->bqhv', p, v)

    starts = jnp.arange(nb, dtype=jnp.int32) * Q_BLOCK
    o = lax.map(block, (to_blocks(q_nope), to_blocks(q_rope), starts))
    return jnp.moveaxis(o, 0, 1).reshape(B, T, H, V_HEAD)


def mla_sample_attend(q_nope, q_rope, c_kv, k_rope, c_past, kr_past, w_uk, w_uv):
    Q = q_nope.shape[1]
    P = c_past.shape[1]
    scale = (QK_NOPE + QK_ROPE) ** -0.5
    q_lat = jnp.einsum('bqhn,chn->bqhc', q_nope, w_uk)
    s_past = jnp.einsum('bqhc,bkc->bhqk', q_lat, c_past) + jnp.einsum('bqhr,bkr->bhqk', q_rope, kr_past)
    s_new = jnp.einsum('bqhc,bkc->bhqk', q_lat, c_kv) + jnp.einsum('bqhr,bkr->bhqk', q_rope, k_rope)
    tri = jnp.tril(jnp.ones((Q, Q), dtype=bool))
    s_new = jnp.where(tri, s_new.astype(jnp.float32) * scale, -jnp.inf)
    s = jnp.concatenate([s_past.astype(jnp.float32) * scale, s_new], axis=-1)
    p = jax.nn.softmax(s, axis=-1).astype(c_kv.dtype)
    o_lat = (jnp.einsum('bhqk,bkc->bqhc', p[..., :P], c_past)
             + jnp.einsum('bhqk,bkc->bqhc', p[..., P:], c_kv))
    return jnp.einsum('bqhc,chv->bqhv', o_lat, w_uv)


def retention(q, k, v, s0):
    B, T, H, _ = q.shape
    C = math.gcd(T, CHUNK)
    lg = jnp.log(1.0 - 2.0 ** (-5.0 - jnp.arange(H, dtype=jnp.float32)))
    idx = jnp.arange(C, dtype=jnp.float32)
    diff = idx[:, None] - idx[None, :]
    dmask = jnp.where(diff >= 0, jnp.exp(lg[:, None, None] * jnp.maximum(diff, 0.0)), 0.0).astype(q.dtype)
    q_dec = jnp.exp(lg[None, :] * (idx[:, None] + 1.0)).astype(q.dtype)
    k_dec = jnp.exp(lg[None, :] * (C - 1.0 - idx[:, None])).astype(q.dtype)
    c_dec = jnp.exp(lg * C).astype(q.dtype)

    def step(S, inp):
        qc, kc, vc = inp
        a = jnp.einsum('bthd,bshd->bhts', qc, kc) * dmask[None]
        o = (jnp.einsum('bhts,bshv->bthv', a, vc)
             + jnp.einsum('bthd,bhdv->bthv', qc, S) * q_dec[None, :, :, None])
        S = S * c_dec[None, :, None, None] + jnp.einsum('bshd,bshv->bhdv', kc * k_dec[None, :, :, None], vc)
        return S, o

    S, o = lax.scan(step, s0, (to_chunks(q, C), to_chunks(k, C), to_chunks(v, C)))
    return from_chunks(o), S


def hgrn2(q, log_f, k, v, s0):
    B, T, H, _ = q.shape
    C = math.gcd(T, CHUNK)
    tri = jnp.tril(jnp.ones((C, C), dtype=bool))[None, :, :, None, None]

    def step(S, inp):
        qc, gc, kc, vc = inp
        b = jnp.cumsum(gc, axis=1)
        diff = b[:, :, None] - b[:, None, :]
        dec = jnp.exp(jnp.where(tri, diff, -jnp.inf)).astype(qc.dtype)
        a = jnp.einsum('bthd,btshd,bshd->bhts', qc, dec, kc)
        o = (jnp.einsum('bhts,bshv->bthv', a, vc)
             + jnp.einsum('bthd,bhdv->bthv', qc * jnp.exp(b).astype(qc.dtype), S))
        bl = b[:, -1]
        S = (S * jnp.exp(bl).astype(S.dtype)[..., None]
             + jnp.einsum('bshd,bshv->bhdv', kc * jnp.exp(bl[:, None] - b).astype(kc.dtype), vc))
        return S, o

    S, o = lax.scan(step, s0, (to_chunks(q, C), to_chunks(log_f, C), to_chunks(k, C), to_chunks(v, C)))
    return from_chunks(o), S


def token_mixers(h, pos, w_in, q_norm, w_uq, kv_norm, w_uk, w_uv, ret_norm, hg_norm, lb,
                 ret_s0, hg_s0, attend):
    B, T, _ = h.shape
    z = h @ w_in
    cq, ckv, kr, rq, rk, rv, rg, gq, gf, gi, gg = split_cols(z)
    cq = rmsnorm(cq, q_norm)
    q = (cq @ w_uq).reshape(B, T, MLA_HEADS, QK_NOPE + QK_ROPE)
    q_nope = q[..., :QK_NOPE]
    q_rope = rope(q[..., QK_NOPE:], pos)
    c_kv = rmsnorm(ckv, kv_norm)
    k_rope = rope(kr[:, :, None, :], pos)[:, :, 0]
    o_mla = attend(q_nope, q_rope, c_kv, k_rope).reshape(B, T, MLA_HEADS * V_HEAD)
    rq = rope(rq.reshape(B, T, RET_HEADS, RET_DK), pos)
    rk = rope(rk.reshape(B, T, RET_HEADS, RET_DK), pos) * (RET_DK ** -0.5)
    o_ret, ret_s = retention(rq, rk, rv.reshape(B, T, RET_HEADS, RET_DV), ret_s0)
    o_ret = head_norm(o_ret, ret_norm, True) * jax.nn.silu(rg)
    zf = gf.reshape(B, T, HG_HEADS, HG_DK).astype(jnp.float32)
    lbh = lb.reshape(HG_HEADS, HG_DK)
    log_f = jnp.log(lbh + (1.0 - lbh) * jax.nn.sigmoid(zf))
    k_in = ((1.0 - lbh) * jax.nn.sigmoid(-zf)).astype(h.dtype)
    q_hg = jax.nn.silu(gq.reshape(B, T, HG_HEADS, HG_DK))
    o_hg, hg_s = hgrn2(q_hg, log_f, k_in, gi.reshape(B, T, HG_HEADS, HG_DV), hg_s0)
    o_hg = head_norm(o_hg, hg_norm, False) * jax.nn.silu(gg)
    mix = jnp.concatenate([o_mla, o_ret, o_hg], axis=-1)
    return mix, c_kv, k_rope, ret_s, hg_s


def peer_block(xb, wq, keys, u, v):
    n = xb.shape[0]
    q = (xb @ wq).reshape(n, PEER_HEADS, 2, PEER_HALF)
    s = jnp.einsum('thpc,hpnc->thpn', q, keys).astype(jnp.float32)
    s1, i1 = lax.top_k(s[:, :, 0], PEER_TOPK)
    s2, i2 = lax.top_k(s[:, :, 1], PEER_TOPK)
    cand = (s1[..., :, None] + s2[..., None, :]).reshape(n, PEER_HEADS, PEER_TOPK * PEER_TOPK)
    cidx = (i1[..., :, None] * PEER_NKEYS + i2[..., None, :]).reshape(n, PEER_HEADS, PEER_TOPK * PEER_TOPK)
    top, sel = lax.top_k(cand, PEER_TOPK)
    eidx = jnp.take_along_axis(cidx, sel, axis=-1)
    g = jax.nn.softmax(top, axis=-1).astype(xb.dtype)
    ue = jnp.take(u, eidx, axis=0)
    a = jax.nn.gelu(jnp.einsum('thkd,td->thk', ue, xb), approximate=False)
    ve = jnp.take(v, eidx, axis=0)
    return jnp.einsum('thk,thkd->td', g * a, ve)


def peer_ffn(h, wq, keys, u, v):
    B, T, D = h.shape
    xt = h.reshape(B * T, D)
    n = B * T
    nb = -(-n // TOK_BLOCK)
    xt = jnp.pad(xt, ((0, nb * TOK_BLOCK - n), (0, 0))).reshape(nb, TOK_BLOCK, D)
    out = lax.map(lambda xb: peer_block(xb, wq, keys, u, v), xt)
    return out.reshape(nb * TOK_BLOCK, D)[:n].reshape(B, T, D)


def setup_inputs(seed: int = 0) -> dict:
    key = jax.random.key(seed)
    ks = jax.random.split(key, 32)
    f32 = jnp.float32

    def nrm(k, shape, s):
        return jax.random.normal(k, shape, f32) * s

    n_pages = PAST_LEN // PAGE_SIZE
    n_used = DEC_BATCH * n_pages
    n_pool = n_used + max(1, n_used // 4)
    perm = jax.random.permutation(ks[0], n_pool)
    page_table = perm[:n_used].reshape(DEC_BATCH, n_pages).astype(jnp.int32)
    return {
        "x_prompt": nrm(ks[1], (BATCH, SEQ, D_MODEL), 1.0),
        "x_sample": nrm(ks[2], (DEC_BATCH, DEC_SEQ, D_MODEL), 1.0),
        "cache_kv_latent": nrm(ks[3], (DEPTH, n_pool, PAGE_SIZE, KV_LORA), 1.0),
        "cache_k_rope": nrm(ks[4], (DEPTH, n_pool, PAGE_SIZE, QK_ROPE), 1.0),
        "state_retention": nrm(ks[5], (DEPTH, DEC_BATCH, RET_HEADS, RET_DK, RET_DV), 0.5),
        "state_hgrn": nrm(ks[6], (DEPTH, DEC_BATCH, HG_HEADS, HG_DK, HG_DV), 0.5),
        "page_table": page_table,
        "w_in": nrm(ks[7], (DEPTH, D_MODEL, IN_COLS), D_MODEL ** -0.5),
        "q_norm": 1.0 + nrm(ks[8], (DEPTH, Q_LORA), 0.05),
        "w_uq": nrm(ks[9], (DEPTH, Q_LORA, MLA_HEADS * (QK_NOPE + QK_ROPE)), Q_LORA ** -0.5),
        "kv_norm": 1.0 + nrm(ks[10], (DEPTH, KV_LORA), 0.05),
        "w_uk": nrm(ks[11], (DEPTH, KV_LORA, MLA_HEADS, QK_NOPE), KV_LORA ** -0.5),
        "w_uv": nrm(ks[12], (DEPTH, KV_LORA, MLA_HEADS, V_HEAD), KV_LORA ** -0.5),
        "ret_norm": 1.0 + nrm(ks[13], (DEPTH, RET_HEADS * RET_DV), 0.05),
        "hg_norm": 1.0 + nrm(ks[14], (DEPTH, HG_HEADS * HG_DV), 0.05),
        "hg_lower_bounds": nrm(ks[15], (DEPTH, HG_HEADS * HG_DK), 0.5),
        "w_out": nrm(ks[16], (DEPTH, MIX_WIDTH, D_MODEL), MIX_WIDTH ** -0.5),
        "attn_norm": 1.0 + nrm(ks[17], (DEPTH, D_MODEL), 0.05),
        "ffn_norm": 1.0 + nrm(ks[18], (DEPTH, D_MODEL), 0.05),
        "final_norm": 1.0 + nrm(ks[19], (D_MODEL,), 0.05),
        "peer_wq": nrm(ks[20], (DEPTH, D_MODEL, PEER_HEADS * PEER_QDIM), D_MODEL ** -0.5),
        "peer_keys": nrm(ks[21], (DEPTH, PEER_HEADS, 2, PEER_NKEYS, PEER_HALF), PEER_HALF ** -0.5),
        "peer_u": nrm(ks[22], (DEPTH, PEER_EXPERTS, D_MODEL), D_MODEL ** -0.5),
        "peer_v": nrm(ks[23], (DEPTH, PEER_EXPERTS, D_MODEL), 0.2),
    }


def reference(x_prompt, x_sample, cache_kv_latent, cache_k_rope, state_retention, state_hgrn, page_table,
              w_in, q_norm, w_uq, kv_norm, w_uk, w_uv, ret_norm, hg_norm, hg_lower_bounds, w_out,
              attn_norm, ffn_norm, final_norm, peer_wq, peer_keys, peer_u, peer_v):
    lb_soft = jax.nn.softmax(hg_lower_bounds.astype(jnp.float32), axis=0)
    lbs = jnp.cumsum(lb_soft, axis=0) - lb_soft[0]
    B, T = x_prompt.shape[:2]
    Bd, Td = x_sample.shape[:2]
    pos_p = jnp.arange(T, dtype=jnp.int32)
    pos_s = PAST_LEN + jnp.arange(Td, dtype=jnp.int32)
    ret0 = jnp.zeros((B, RET_HEADS, RET_DK, RET_DV), x_prompt.dtype)
    hg0 = jnp.zeros((B, HG_HEADS, HG_DK, HG_DV), x_prompt.dtype)
    xp, xs = x_prompt, x_sample
    p_ckv, p_kr, p_ret, p_hg = [], [], [], []
    s_ckv, s_kr, s_ret, s_hg = [], [], [], []
    for l in range(DEPTH):
        attend_p = functools.partial(mla_prompt_attend, w_uk=w_uk[l], w_uv=w_uv[l])
        mix, ckv, kr, rs, hs = token_mixers(rmsnorm(xp, attn_norm[l]), pos_p, w_in[l], q_norm[l], w_uq[l],
                                            kv_norm[l], w_uk[l], w_uv[l], ret_norm[l], hg_norm[l], lbs[l],
                                            ret0, hg0, attend_p)
        xp = xp + mix @ w_out[l]
        xp = xp + peer_ffn(rmsnorm(xp, ffn_norm[l]), peer_wq[l], peer_keys[l], peer_u[l], peer_v[l])
        p_ckv.append(ckv); p_kr.append(kr); p_ret.append(rs); p_hg.append(hs)
        c_past = cache_kv_latent[l][page_table].reshape(Bd, -1, KV_LORA)
        kr_past = cache_k_rope[l][page_table].reshape(Bd, -1, QK_ROPE)
        attend_s = functools.partial(mla_sample_attend, c_past=c_past, kr_past=kr_past, w_uk=w_uk[l], w_uv=w_uv[l])
        mix, ckv, kr, rs, hs = token_mixers(rmsnorm(xs, attn_norm[l]), pos_s, w_in[l], q_norm[l], w_uq[l],
                                            kv_norm[l], w_uk[l], w_uv[l], ret_norm[l], hg_norm[l], lbs[l],
                                            state_retention[l], state_hgrn[l], attend_s)
        xs = xs + mix @ w_out[l]
        xs = xs + peer_ffn(rmsnorm(xs, ffn_norm[l]), peer_wq[l], peer_keys[l], peer_u[l], peer_v[l])
        s_ckv.append(ckv); s_kr.append(kr); s_ret.append(rs); s_hg.append(hs)
    y_prompt = rmsnorm(xp, final_norm)
    y_sample = rmsnorm(xs, final_norm)
    return (y_prompt, y_sample,
            jnp.stack(p_ckv), jnp.stack(p_kr), jnp.stack(p_ret), jnp.stack(p_hg),
            jnp.stack(s_ckv), jnp.stack(s_kr), jnp.stack(s_ret), jnp.stack(s_hg))
```

```python
import functools
import math

import jax
import jax.numpy as jnp
from jax import lax
from jax.experimental import pallas as pl
from jax.experimental.pallas import tpu as pltpu

D_MODEL = 1024
MLA_HEADS = 8
QK_NOPE = 64
QK_ROPE = 32
V_HEAD = 64
Q_LORA = 256
KV_LORA = 128
RET_HEADS = 4
RET_DK = 64
RET_DV = 64
HG_HEADS = 4
HG_DK = 64
HG_DV = 64
PEER_HEADS = 8
PEER_NKEYS = 128
PEER_HALF = 128
PEER_TOPK = 16
ROPE_BASE = 10000.0
EPS = 1e-6
PAGE_SIZE = 128

LANES = 128
HEAD_SLOT = 128
REC_W = 256
NEG_INF = float("-inf")
VMEM_LIMIT = 56 * 1024 * 1024

ZP_RQ, ZP_RK, ZP_RV, ZP_RG, ZP_GQ, ZP_GF, ZP_GK, ZP_GI, ZP_GG = range(9)
ZP_GROUPS = 9


def _bf(x):
    return x.astype(jnp.bfloat16)


def _dot(a, b):
    return jnp.dot(a, b, preferred_element_type=jnp.float32)


def _dot_nt(a, b):
    return lax.dot_general(a, b, (((1,), (1,)), ((), ())), preferred_element_type=jnp.float32)


def _dot_tn(a, b):
    return lax.dot_general(a, b, (((0,), (0,)), ((), ())), preferred_element_type=jnp.float32)


def _rms(x, g):
    return x * lax.rsqrt(jnp.mean(x * x, axis=-1, keepdims=True) + EPS) * g


def _silu(x):
    return x * (1.0 / (1.0 + jnp.exp(-x)))


def _split3(x):
    x1 = _bf(x)
    r1 = x - x1.astype(jnp.float32)
    x2 = _bf(r1)
    x3 = _bf(r1 - x2.astype(jnp.float32))
    return x1, x2, x3


def _proj_in_kernel(x_ref, gattn_ref, wbig_ref, qn_ref, kvn_ref, wuq_ref, wuk_ref, wuv_ref, lb_ref,
                    cq_ref, sq_ref, c64_ref, s64_ref,
                    ckv_ref, kr_ref, q_ref, k_ref, v_ref, zp_ref):
    scale = (QK_NOPE + QK_ROPE) ** -0.5
    h = _bf(_rms(x_ref[...], gattn_ref[...]))
    z = _dot(h, wbig_ref[...])
    o = 0
    cq = z[:, o:o + Q_LORA]; o += Q_LORA
    ckv = z[:, o:o + KV_LORA]; o += KV_LORA
    krp = z[:, o:o + HEAD_SLOT]; o += HEAD_SLOT
    krp_rot = z[:, o:o + HEAD_SLOT]; o += HEAD_SLOT
    groups = []
    for _ in range(10):
        groups.append(z[:, o:o + REC_W]); o += REC_W
    rq, rq_rot, rk, rk_rot, rv, rg, gq, gf, gi, gg = groups

    cq_t = cq_ref[...]
    sq_t = sq_ref[...]
    cqn = _bf(_rms(cq, qn_ref[...]))
    zq = _dot(cqn, wuq_ref[...])
    hw = MLA_HEADS * HEAD_SLOT
    for hd in range(MLA_HEADS):
        a = zq[:, hd * HEAD_SLOT:(hd + 1) * HEAD_SLOT]
        b = zq[:, hw + hd * HEAD_SLOT:hw + (hd + 1) * HEAD_SLOT]
        q_ref[hd] = _bf((a * cq_t + b * sq_t) * scale)
    ckvn = _rms(ckv, kvn_ref[...])
    ckv_ref[...] = ckvn
    kr_placed = krp * cq_t + krp_rot * sq_t
    kr_ref[...] = kr_placed[:, QK_NOPE:QK_NOPE + QK_ROPE]
    ckvb = _bf(ckvn)
    kn = _dot(ckvb, wuk_ref[...])
    for hd in range(MLA_HEADS):
        k_ref[hd] = _bf(kn[:, hd * HEAD_SLOT:(hd + 1) * HEAD_SLOT] + kr_placed)
    vv = _dot(ckvb, wuv_ref[...])
    for p in range(MLA_HEADS // 2):
        v_ref[p] = _bf(vv[:, p * LANES:(p + 1) * LANES])

    c64 = c64_ref[...]
    s64 = s64_ref[...]
    zp_ref[:, ZP_RQ * REC_W:(ZP_RQ + 1) * REC_W] = rq * c64 + rq_rot * s64
    zp_ref[:, ZP_RK * REC_W:(ZP_RK + 1) * REC_W] = (rk * c64 + rk_rot * s64) * (RET_DK ** -0.5)
    zp_ref[:, ZP_RV * REC_W:(ZP_RV + 1) * REC_W] = rv
    zp_ref[:, ZP_RG * REC_W:(ZP_RG + 1) * REC_W] = rg
    lb = lb_ref[...]
    sig = 1.0 / (1.0 + jnp.exp(-gf))
    zp_ref[:, ZP_GQ * REC_W:(ZP_GQ + 1) * REC_W] = _silu(gq)
    zp_ref[:, ZP_GF * REC_W:(ZP_GF + 1) * REC_W] = jnp.log(lb + (1.0 - lb) * sig)
    zp_ref[:, ZP_GK * REC_W:(ZP_GK + 1) * REC_W] = (1.0 - lb) * (1.0 / (1.0 + jnp.exp(gf)))
    zp_ref[:, ZP_GI * REC_W:(ZP_GI + 1) * REC_W] = gi
    zp_ref[:, ZP_GG * REC_W:(ZP_GG + 1) * REC_W] = gg


def _rot_cols(w, n_heads, d):
    k = w.shape[0]
    w4 = w.reshape(k, n_heads, 2, d // 2)
    return jnp.concatenate([-w4[:, :, 1:2], w4[:, :, 0:1]], axis=2).reshape(k, n_heads * d)


def _prep_layer_weights(w_in, w_uq, w_uk, w_uv):
    sizes = (Q_LORA, KV_LORA, QK_ROPE) + (REC_W,) * 8
    offs = [0]
    for s in sizes:
        offs.append(offs[-1] + s)
    cols = [w_in[:, offs[i]:offs[i + 1]] for i in range(len(sizes))]
    cq, ckv, kr, rq, rk, rv, rg, gq, gf, gi, gg = cols
    kdim = w_in.shape[0]
    zeros = lambda n: jnp.zeros((kdim, n), w_in.dtype)
    place = lambda c: jnp.concatenate([zeros(QK_NOPE), c, zeros(HEAD_SLOT - QK_NOPE - QK_ROPE)], axis=1)
    wbig = jnp.concatenate([
        cq, ckv, place(kr), place(_rot_cols(kr, 1, QK_ROPE)),
        rq, _rot_cols(rq, RET_HEADS, RET_DK), rk, _rot_cols(rk, RET_HEADS, RET_DK),
        rv, rg, gq, gf, gi, gg], axis=1)
    wq3 = w_uq.reshape(Q_LORA, MLA_HEADS, QK_NOPE + QK_ROPE)
    nope, ropep = wq3[..., :QK_NOPE], wq3[..., QK_NOPE:]
    rope_rot = jnp.concatenate([-ropep[..., QK_ROPE // 2:], ropep[..., :QK_ROPE // 2]], axis=-1)
    pad = jnp.zeros((Q_LORA, MLA_HEADS, HEAD_SLOT - QK_NOPE - QK_ROPE), w_uq.dtype)
    plain = jnp.concatenate([nope, ropep, pad], axis=-1).reshape(Q_LORA, MLA_HEADS * HEAD_SLOT)
    rotd = jnp.concatenate([jnp.zeros_like(nope), rope_rot, pad], axis=-1).reshape(Q_LORA, MLA_HEADS * HEAD_SLOT)
    wuq = jnp.concatenate([plain, rotd], axis=1)
    wuk = jnp.concatenate([w_uk, jnp.zeros((KV_LORA, MLA_HEADS, HEAD_SLOT - QK_NOPE), w_uk.dtype)],
                          axis=-1).reshape(KV_LORA, MLA_HEADS * HEAD_SLOT)
    wuv = w_uv.reshape(KV_LORA, MLA_HEADS * V_HEAD)
    return _bf(wbig), _bf(wuq), _bf(wuk), _bf(wuv)


def _rope_tables(pos):
    posf = pos.astype(jnp.float32)[:, None]

    def cs(d):
        half = d // 2
        inv = ROPE_BASE ** (-jnp.arange(half, dtype=jnp.float32) / half)
        ang = posf * inv[None, :]
        return jnp.cos(ang), jnp.sin(ang)

    n = pos.shape[0]
    c32, s32 = cs(QK_ROPE)
    tail = jnp.zeros((n, HEAD_SLOT - QK_NOPE - QK_ROPE), jnp.float32)
    cq = jnp.concatenate([jnp.ones((n, QK_NOPE), jnp.float32), c32, c32, tail], axis=1)
    sq = jnp.concatenate([jnp.zeros((n, QK_NOPE), jnp.float32), s32, s32, tail], axis=1)
    c64, s64 = cs(RET_DK)
    c64 = jnp.tile(jnp.concatenate([c64, c64], axis=1), (1, RET_HEADS))
    s64 = jnp.tile(jnp.concatenate([s64, s64], axis=1), (1, RET_HEADS))
    return cq, sq, c64, s64


def _proj_in(x, gattn, wbig, qn, kvn, wuq, wuk, wuv, lb, tabs, tm):
    n = x.shape[0]
    cq_t, sq_t, c64, s64 = tabs
    full = lambda a: pl.BlockSpec(a.shape, lambda i: (0,) * a.ndim)
    row = lambda w: pl.BlockSpec((tm, w), lambda i: (i, 0))
    hrow = lambda nh: pl.BlockSpec((nh, tm, HEAD_SLOT), lambda i: (0, i, 0))
    out_shape = (
        jax.ShapeDtypeStruct((n, KV_LORA), jnp.float32),
        jax.ShapeDtypeStruct((n, QK_ROPE), jnp.float32),
        jax.ShapeDtypeStruct((MLA_HEADS, n, HEAD_SLOT), jnp.bfloat16),
        jax.ShapeDtypeStruct((MLA_HEADS, n, HEAD_SLOT), jnp.bfloat16),
        jax.ShapeDtypeStruct((MLA_HEADS // 2, n, LANES), jnp.bfloat16),
        jax.ShapeDtypeStruct((n, ZP_GROUPS * REC_W), jnp.float32),
    )
    return pl.pallas_call(
        _proj_in_kernel,
        grid=(n // tm,),
        in_specs=[row(D_MODEL), full(gattn), full(wbig), full(qn), full(kvn), full(wuq), full(wuk), full(wuv),
                  full(lb), row(HEAD_SLOT), row(HEAD_SLOT), row(REC_W), row(REC_W)],
        out_specs=(row(KV_LORA), row(QK_ROPE), hrow(MLA_HEADS), hrow(MLA_HEADS), hrow(MLA_HEADS // 2),
                   row(ZP_GROUPS * REC_W)),
        out_shape=out_shape,
        compiler_params=pltpu.CompilerParams(dimension_semantics=("parallel",), vmem_limit_bytes=VMEM_LIMIT),
        name="proj_in",
    )(x, gattn, wbig, qn, kvn, wuq, wuk, wuv, lb, cq_t, sq_t, c64, s64)


def _softmax_step(s, vb, m, l, acc):
    m_new = jnp.maximum(m, jnp.max(s, axis=-1, keepdims=True))
    alpha = jnp.exp(m - m_new)
    p = jnp.exp(s - m_new)
    l = alpha * l + jnp.sum(p, axis=-1, keepdims=True)
    acc = alpha * acc + _dot(_bf(p), vb)
    return m_new, l, acc


def _attn_prompt_kernel(q_ref, k_ref, v_ref, o_ref, *, blk):
    qi = pl.program_id(2)
    row = lax.broadcasted_iota(jnp.int32, (blk, blk), 0)
    col = lax.broadcasted_iota(jnp.int32, (blk, blk), 1)
    outs = []
    for hh in range(2):
        q = q_ref[hh]

        def body(j, carry, hh=hh, q=q):
            start = pl.multiple_of(j * blk, blk)
            kb = k_ref[hh, pl.ds(start, blk), :]
            vb = v_ref[pl.ds(start, blk), :]
            return _softmax_step(_dot_nt(q, kb), vb, *carry)

        init = (jnp.full((blk, 1), NEG_INF, jnp.float32), jnp.zeros((blk, 1), jnp.float32),
                jnp.zeros((blk, LANES), jnp.float32))
        m, l, acc = lax.fori_loop(0, qi, body, init)
        start = pl.multiple_of(qi * blk, blk)
        kb = k_ref[hh, pl.ds(start, blk), :]
        vb = v_ref[pl.ds(start, blk), :]
        s = jnp.where(col <= row, _dot_nt(q, kb), NEG_INF)
        m, l, acc = _softmax_step(s, vb, m, l, acc)
        outs.append(acc / l)
    lane = lax.broadcasted_iota(jnp.int32, (blk, LANES), 1)
    o_ref[...] = _bf(jnp.where(lane < V_HEAD, outs[0], outs[1]))


def _attn_prompt(q, k, v, batch, seq, blk):
    nq = seq // blk
    return pl.pallas_call(
        functools.partial(_attn_prompt_kernel, blk=blk),
        grid=(batch, MLA_HEADS // 2, nq),
        in_specs=[pl.BlockSpec((2, blk, HEAD_SLOT), lambda b, p, i: (p, b * nq + i, 0)),
                  pl.BlockSpec((2, seq, HEAD_SLOT), lambda b, p, i: (p, b, 0)),
                  pl.BlockSpec((None, seq, LANES), lambda b, p, i: (p, b, 0))],
        out_specs=pl.BlockSpec((blk, LANES), lambda b, p, i: (b * nq + i, p)),
        out_shape=jax.ShapeDtypeStruct((batch * seq, MLA_HEADS * V_HEAD), jnp.bfloat16),
        compiler_params=pltpu.CompilerParams(dimension_semantics=("parallel", "parallel", "arbitrary"),
                                             vmem_limit_bytes=VMEM_LIMIT),
        name="attn_prompt",
    )(q, k, v)


def _attn_sample_kernel(pt_ref, q_ref, wuk_ref, wuvp_ref, ckv_ref, kr_ref, *rest, pages_per_step, td):
    kv_refs = rest[:pages_per_step]
    krp_refs = rest[pages_per_step:2 * pages_per_step]
    o_ref = rest[2 * pages_per_step]
    qlat_ref, qrope_ref, m_ref, l_ref, acc_ref = rest[2 * pages_per_step + 1:]
    j = pl.program_id(1)
    rows = MLA_HEADS * td

    @pl.when(j == 0)
    def _():
        for hd in range(MLA_HEADS):
            qh = q_ref[hd].astype(jnp.float32)
            qlat_ref[hd * td:(hd + 1) * td, :] = _dot_nt(_bf(qh[:, :QK_NOPE]), wuk_ref[hd])
            qrope_ref[hd * td:(hd + 1) * td, :] = qh[:, QK_NOPE:QK_NOPE + QK_ROPE]
        m_ref[...] = jnp.full((rows, 1), NEG_INF, jnp.float32)
        l_ref[...] = jnp.zeros((rows, 1), jnp.float32)
        acc_ref[...] = jnp.zeros((rows, KV_LORA), jnp.float32)

    qlat = _bf(qlat_ref[...])
    qrope = _bf(qrope_ref[...])
    m, l, acc = m_ref[...], l_ref[...], acc_ref[...]
    for r in range(pages_per_step):
        page = _bf(kv_refs[r][...])
        s = _dot_nt(qlat, page) + _dot_nt(qrope, _bf(krp_refs[r][...]))
        m, l, acc = _softmax_step(s, page, m, l, acc)
    m_ref[...], l_ref[...], acc_ref[...] = m, l, acc

    @pl.when(j == pl.num_programs(1) - 1)
    def _():
        pad = jnp.zeros((PAGE_SIZE - td, KV_LORA), jnp.float32)
        cnew = _bf(jnp.concatenate([ckv_ref[...], pad], axis=0))
        krnew = _bf(jnp.concatenate([kr_ref[...], pad[:, :QK_ROPE]], axis=0))
        s = _dot_nt(qlat, cnew) + _dot_nt(qrope, krnew)
        row = lax.broadcasted_iota(jnp.int32, (rows, PAGE_SIZE), 0)
        col = lax.broadcasted_iota(jnp.int32, (rows, PAGE_SIZE), 1)
        s = jnp.where(col <= row % td, s, NEG_INF)
        m2, l2, acc2 = _softmax_step(s, cnew, m, l, acc)
        olat = _bf(acc2 / l2)
        out = jnp.zeros((td, MLA_HEADS * V_HEAD), jnp.float32)
        for hd in range(MLA_HEADS):
            out = out + _dot(olat[hd * td:(hd + 1) * td, :], wuvp_ref[hd])
        o_ref[...] = out


def _attn_sample(page_table, q, wuk_h, wuv_placed, ckv, kr, cache_kv, cache_kr, layer, row0, dec_batch, td,
                 pages_per_step):
    n_pages = page_table.shape[1]
    steps = n_pages // pages_per_step
    blk0 = row0 // td
    rows = MLA_HEADS * td

    def page_spec(width, r):
        return pl.BlockSpec((None, None, PAGE_SIZE, width),
                            lambda b, j, pt, r=r: (layer, pt[b * n_pages + j * pages_per_step + r], 0, 0))

    in_specs = [pl.BlockSpec((MLA_HEADS, td, HEAD_SLOT), lambda b, j, pt: (0, blk0 + b, 0)),
                pl.BlockSpec(wuk_h.shape, lambda b, j, pt: (0, 0, 0)),
                pl.BlockSpec(wuv_placed.shape, lambda b, j, pt: (0, 0, 0)),
                pl.BlockSpec((td, KV_LORA), lambda b, j, pt: (blk0 + b, 0)),
                pl.BlockSpec((td, QK_ROPE), lambda b, j, pt: (blk0 + b, 0))]
    in_specs += [page_spec(KV_LORA, r) for r in range(pages_per_step)]
    in_specs += [page_spec(QK_ROPE, r) for r in range(pages_per_step)]
    grid_spec = pltpu.PrefetchScalarGridSpec(
        num_scalar_prefetch=1,
        grid=(dec_batch, steps),
        in_specs=in_specs,
        out_specs=pl.BlockSpec((td, MLA_HEADS * V_HEAD), lambda b, j, pt: (b, 0)),
        scratch_shapes=[pltpu.VMEM((rows, KV_LORA), jnp.float32), pltpu.VMEM((rows, QK_ROPE), jnp.float32),
                        pltpu.VMEM((rows, 1), jnp.float32), pltpu.VMEM((rows, 1), jnp.float32),
                        pltpu.VMEM((rows, KV_LORA), jnp.float32)])
    return pl.pallas_call(
        functools.partial(_attn_sample_kernel, pages_per_step=pages_per_step, td=td),
        grid_spec=grid_spec,
        out_shape=jax.ShapeDtypeStruct((dec_batch * td, MLA_HEADS * V_HEAD), jnp.float32),
        compiler_params=pltpu.CompilerParams(dimension_semantics=("parallel", "arbitrary"),
                                             vmem_limit_bytes=VMEM_LIMIT),
        name="attn_sample",
    )(page_table.reshape(-1), q, wuk_h, wuv_placed, ckv, kr, *([cache_kv] * pages_per_step),
      *([cache_kr] * pages_per_step))


def _place_wuv(w_uv):
    eye = jnp.eye(MLA_HEADS, dtype=w_uv.dtype)
    placed = jnp.einsum('chv,hg->hcgv', w_uv, eye)
    return _bf(placed.reshape(MLA_HEADS, KV_LORA, MLA_HEADS * V_HEAD))


def _ret_gammas():
    return [1.0 - 2.0 ** (-5.0 - h) for h in range(RET_HEADS)]


def _ret_kernel(q_ref, k_ref, v_ref, g_ref, s0_ref, dmask_ref, qdec_ref, kdec_ref, gn_ref,
                o_ref, sfin_ref, s_ref, *, nb, decay_len):
    c = pl.program_id(1)

    @pl.when(c == 0)
    def _():
        s_ref[...] = s0_ref[...]

    qdec = qdec_ref[...]
    kdec = kdec_ref[...]
    gn = gn_ref[...]
    for bi in range(nb):
        q = q_ref[bi]
        k = k_ref[bi]
        v = v_ref[bi]
        kd = k * kdec
        gate = _silu(g_ref[bi])
        for h in range(RET_HEADS):
            sl = slice(h * RET_DK, (h + 1) * RET_DK)
            qh, kh, vh = _bf(q[:, sl]), _bf(k[:, sl]), _bf(v[:, sl])
            a = _dot_nt(qh, kh) * dmask_ref[h]
            s = s_ref[bi, h]
            o = _dot(_bf(a), vh) + _dot(qh, _bf(s)) * qdec[:, sl]
            s_ref[bi, h] = s * (_ret_gammas()[h] ** decay_len) + _dot_tn(_bf(kd[:, sl]), vh)
            oc = o - jnp.mean(o, axis=-1, keepdims=True)
            y = oc * lax.rsqrt(jnp.mean(oc * oc, axis=-1, keepdims=True) + EPS)
            o_ref[bi, :, sl] = _bf(y * gn[:, sl] * gate[:, sl])

    @pl.when(c == pl.num_programs(1) - 1)
    def _():
        sfin_ref[...] = s_ref[...]


def _ret_consts(chunk, decay_len):
    lg = jnp.log(jnp.asarray(_ret_gammas(), jnp.float32))
    idx = jnp.arange(chunk, dtype=jnp.float32)
    diff = idx[:, None] - idx[None, :]
    dmask = jnp.where(diff >= 0, jnp.exp(lg[:, None, None] * jnp.maximum(diff, 0.0)), 0.0)
    qdec = jnp.exp(lg[None, :] * (idx[:, None] + 1.0))
    kdec = jnp.exp(lg[None, :] * (decay_len - 1.0 - idx[:, None]))
    rep = lambda a: jnp.repeat(a, RET_DK, axis=1)
    return dmask, rep(qdec), rep(kdec)


def _rec_specs(nb, chunk, nch, groups):
    return [pl.BlockSpec((nb, chunk, REC_W), lambda b, c, g=g: (b * nch + c, 0, g)) for g in groups]


def _retention(zp3, s0, gn, nb, nch, decay_len):
    chunk = zp3.shape[1]
    batch = s0.shape[0]
    dmask, qdec, kdec = _ret_consts(chunk, decay_len)
    full = lambda a: pl.BlockSpec(a.shape, lambda b, c: (0,) * a.ndim)
    st = pl.BlockSpec((nb, RET_HEADS, RET_DK, RET_DV), lambda b, c: (b, 0, 0, 0))
    return pl.pallas_call(
        functools.partial(_ret_kernel, nb=nb, decay_len=decay_len),
        grid=(batch // nb, nch),
        in_specs=_rec_specs(nb, chunk, nch, (ZP_RQ, ZP_RK, ZP_RV, ZP_RG)) + [st, full(dmask), full(qdec),
                                                                              full(kdec), full(gn)],
        out_specs=(pl.BlockSpec((nb, chunk, REC_W), lambda b, c: (b * nch + c, 0, 0)), st),
        out_shape=(jax.ShapeDtypeStruct((batch * nch, chunk, REC_W), jnp.bfloat16),
                   jax.ShapeDtypeStruct(s0.shape, jnp.float32)),
        scratch_shapes=[pltpu.VMEM((nb, RET_HEADS, RET_DK, RET_DV), jnp.float32)],
        compiler_params=pltpu.CompilerParams(dimension_semantics=("parallel", "arbitrary"),
                                             vmem_limit_bytes=VMEM_LIMIT),
        name="retention",
    )(zp3, zp3, zp3, zp3, s0, dmask, qdec, kdec, gn)


def _hgrn_sum_matrix(chunk):
    import numpy as np
    t = np.arange(chunk)[:, None]
    r = np.arange(chunk)[None, :]
    blocks = [r <= t, r > t]
    m = 2
    while m <= chunk:
        half = m // 2
        mid = (t // m) * m + half
        upper = (t % m) >= half
        blocks.append(upper & (r >= mid) & (r <= t))
        blocks.append(~upper & (r > t) & (r <= mid - 1))
        m *= 2
    return jnp.asarray(np.concatenate(blocks, axis=0).astype(np.float32), jnp.bfloat16)


def _hgrn_kernel(q_ref, f_ref, k_ref, v_ref, g_ref, s0_ref, msum_ref, gn_ref, o_ref, sfin_ref, st_ref, *,
                 nb, chunk):
    c = pl.program_id(1)
    nlev = chunk.bit_length() - 1

    @pl.when(c == 0)
    def _():
        for bi in range(nb):
            for h in range(HG_HEADS):
                st_ref[bi, h] = s0_ref[bi, h].T

    gn = gn_ref[...]
    rowi = lax.broadcasted_iota(jnp.int32, (chunk, REC_W), 0)
    r2 = lax.broadcasted_iota(jnp.int32, (chunk, chunk), 0)
    c2 = lax.broadcasted_iota(jnp.int32, (chunk, chunk), 1)
    for bi in range(nb):
        q = q_ref[bi]
        k = k_ref[bi]
        v = v_ref[bi]
        gate = _silu(g_ref[bi])
        f1, f2, f3 = _split3(f_ref[bi])
        sums = _dot(msum_ref[...], jnp.concatenate([f1, f2, f3], axis=1))
        sums = sums[:, :REC_W] + sums[:, REC_W:2 * REC_W] + sums[:, 2 * REC_W:]
        part = lambda i: sums[i * chunk:(i + 1) * chunk]
        b = part(0)
        qb = _bf(q * jnp.exp(b))
        ks = _bf(k * jnp.exp(part(1)))
        e_last = jnp.exp(b[chunk - 1:chunk, :])
        vb = _bf(v)
        qs, kls = [_bf(q)], [_bf(k)]
        for lev in range(1, nlev + 1):
            upper = (rowi & (1 << (lev - 1))) != 0
            qs.append(_bf(jnp.where(upper, q * jnp.exp(part(2 * lev)), 0.0)))
            kls.append(_bf(jnp.where(upper, 0.0, k * jnp.exp(part(2 * lev + 1)))))
        for h in range(HG_HEADS):
            sl = slice(h * HG_DK, (h + 1) * HG_DK)
            a = jnp.where(r2 == c2, _dot_nt(qs[0][:, sl], kls[0][:, sl]), 0.0)
            for lev in range(1, nlev + 1):
                same = (r2 >> lev) == (c2 >> lev)
                a = a + jnp.where(same, _dot_nt(qs[lev][:, sl], kls[lev][:, sl]), 0.0)
            st = st_ref[bi, h]
            o = _dot(_bf(a), vb[:, sl]) + _dot_nt(qb[:, sl], _bf(st))
            st_ref[bi, h] = st * e_last[:, sl] + _dot_tn(vb[:, sl], ks[:, sl])
            y = o * lax.rsqrt(jnp.mean(o * o, axis=-1, keepdims=True) + EPS)
            o_ref[bi, :, sl] = _bf(y * gn[:, sl] * gate[:, sl])

    @pl.when(c == pl.num_programs(1) - 1)
    def _():
        for bi in range(nb):
            for h in range(HG_HEADS):
                sfin_ref[bi, h] = st_ref[bi, h].T


def _hgrn(zp3, s0, gn, nb, nch):
    chunk = zp3.shape[1]
    batch = s0.shape[0]
    msum = _hgrn_sum_matrix(chunk)
    full = lambda a: pl.BlockSpec(a.shape, lambda b, c: (0,) * a.ndim)
    st = pl.BlockSpec((nb, HG_HEADS, HG_DK, HG_DV), lambda b, c: (b, 0, 0, 0))
    return pl.pallas_call(
        functools.partial(_hgrn_kernel, nb=nb, chunk=chunk),
        grid=(batch // nb, nch),
        in_specs=_rec_specs(nb, chunk, nch, (ZP_GQ, ZP_GF, ZP_GK, ZP_GI, ZP_GG)) + [st, full(msum), full(gn)],
        out_specs=(pl.BlockSpec((nb, chunk, REC_W), lambda b, c: (b * nch + c, 0, 0)), st),
        out_shape=(jax.ShapeDtypeStruct((batch * nch, chunk, REC_W), jnp.bfloat16),
                   jax.ShapeDtypeStruct(s0.shape, jnp.float32)),
        scratch_shapes=[pltpu.VMEM((nb, HG_HEADS, HG_DV, HG_DK), jnp.float32)],
        compiler_params=pltpu.CompilerParams(dimension_semantics=("parallel", "arbitrary"),
                                             vmem_limit_bytes=VMEM_LIMIT),
        name="hgrn2",
    )(zp3, zp3, zp3, zp3, zp3, s0, msum, gn)


def _dot3(ah, al, bh, bl, dot):
    return dot(ah, bh) + dot(al, bh) + dot(ah, bl)


def _split2(x):
    hi = _bf(x)
    return hi, _bf(x - hi.astype(jnp.float32))


def _out_proj_kernel(x_ref, omla_ref, oret_ref, ohg_ref, wout_ref, gffn_ref, wqh_ref, wql_ref, kh_ref, kl_ref,
                     x1_ref, h2t_ref, st_ref):
    mix = jnp.concatenate([omla_ref[...], oret_ref[...], ohg_ref[...]], axis=1)
    x1 = x_ref[...] + _dot(mix, wout_ref[...])
    x1_ref[...] = x1
    h2 = _rms(x1, gffn_ref[...])
    h2t_ref[...] = _bf(h2.T)
    hh, hl = _split2(h2)
    q = _dot3(hh, hl, wqh_ref[...], wql_ref[...], _dot)
    for p in range(2 * PEER_HEADS):
        qh, ql = _split2(q[:, p * PEER_HALF:(p + 1) * PEER_HALF])
        st_ref[p] = _dot3(kh_ref[p], kl_ref[p], qh, ql, _dot_nt)


def _out_proj(x, omla, oret, ohg, wout, gffn, wqh, wql, kh, kl, tm):
    n = x.shape[0]
    full = lambda a: pl.BlockSpec(a.shape, lambda i: (0,) * a.ndim)
    row = lambda w: pl.BlockSpec((tm, w), lambda i: (i, 0))
    return pl.pallas_call(
        _out_proj_kernel,
        grid=(n // tm,),
        in_specs=[row(D_MODEL), row(MLA_HEADS * V_HEAD), row(REC_W), row(REC_W), full(wout), full(gffn), full(wqh),
                  full(wql), full(kh), full(kl)],
        out_specs=(row(D_MODEL), pl.BlockSpec((D_MODEL, tm), lambda i: (0, i)),
                   pl.BlockSpec((2 * PEER_HEADS, PEER_NKEYS, tm), lambda i: (0, 0, i))),
        out_shape=(jax.ShapeDtypeStruct((n, D_MODEL), jnp.float32),
                   jax.ShapeDtypeStruct((D_MODEL, n), jnp.bfloat16),
                   jax.ShapeDtypeStruct((2 * PEER_HEADS, PEER_NKEYS, n), jnp.float32)),
        compiler_params=pltpu.CompilerParams(dimension_semantics=("parallel",), vmem_limit_bytes=VMEM_LIMIT),
        name="out_proj_peer_scores",
    )(x, omla, oret, ohg, wout, gffn, wqh, wql, kh, kl)


def _extract_top(x, count, rowf, on_value):
    for r in range(count):
        m = jnp.max(x, axis=0, keepdims=True)
        on_value(r, m)
        first = jnp.min(jnp.where(x == m, rowf, float(x.shape[0])), axis=0, keepdims=True)
        x = jnp.where(rowf == first, NEG_INF, x)
    return x


def _peer_select_kernel(s_ref, theta_ref, coef_ref, e2_ref, v1_ref, v2_ref, cand_ref):
    s1 = s_ref[0]
    s2 = s_ref[1]
    rowf = lax.broadcasted_iota(jnp.int32, s1.shape, 0).astype(jnp.float32)

    def keep(ref):
        def on_value(r, m):
            ref[r:r + 1, :] = m
        return on_value

    _extract_top(s1, PEER_TOPK + 1, rowf, keep(v1_ref))
    _extract_top(s2, PEER_TOPK + 1, rowf, keep(v2_ref))
    v2_top = v2_ref[0:PEER_TOPK, :]
    for r1 in range(PEER_TOPK):
        cand_ref[r1 * PEER_TOPK:(r1 + 1) * PEER_TOPK, :] = v1_ref[r1:r1 + 1, :] + v2_top
    cand = cand_ref[...]
    rowc = lax.broadcasted_iota(jnp.int32, cand.shape, 0).astype(jnp.float32)
    best = v1_ref[0:1, :] + v2_ref[0:1, :]
    picked = []
    rest = _extract_top(cand, PEER_TOPK, rowc, lambda r, m: picked.append(m))
    z = picked[0] * 0.0
    for m in picked:
        z = z + jnp.exp(m - best)
    nxt = jnp.max(rest, axis=0, keepdims=True)
    nxt = jnp.maximum(nxt, v1_ref[PEER_TOPK:PEER_TOPK + 1, :] + v2_ref[0:1, :])
    nxt = jnp.maximum(nxt, v1_ref[0:1, :] + v2_ref[PEER_TOPK:PEER_TOPK + 1, :])
    tau = 0.5 * (picked[-1] + nxt)
    theta_ref[...] = tau - s1
    coef_ref[...] = jnp.exp(s1 - v1_ref[0:1, :]) * (1.0 / z)
    e2_ref[...] = jnp.exp(s2 - v2_ref[0:1, :])


def _peer_select(st, lanes):
    n = st.shape[-1]
    s4 = st.reshape(PEER_HEADS, 2, PEER_NKEYS, n)
    out = pl.BlockSpec((None, PEER_NKEYS, lanes), lambda h, t: (h, 0, t))
    shape = jax.ShapeDtypeStruct((PEER_HEADS, PEER_NKEYS, n), jnp.float32)
    vrows = 8 * ((PEER_TOPK + 1 + 7) // 8)
    return pl.pallas_call(
        _peer_select_kernel,
        grid=(PEER_HEADS, n // lanes),
        in_specs=[pl.BlockSpec((None, 2, PEER_NKEYS, lanes), lambda h, t: (h, 0, 0, t))],
        out_specs=(out, out, out),
        out_shape=(shape, shape, shape),
        scratch_shapes=[pltpu.VMEM((vrows, lanes), jnp.float32), pltpu.VMEM((vrows, lanes), jnp.float32),
                        pltpu.VMEM((PEER_TOPK * PEER_TOPK, lanes), jnp.float32)],
        compiler_params=pltpu.CompilerParams(dimension_semantics=("parallel", "parallel"),
                                             vmem_limit_bytes=VMEM_LIMIT),
        name="peer_select",
    )(s4)


def _peer_expert_kernel(h2t_ref, u_ref, vt_ref, s2_ref, e2_ref, theta_ref, coef_ref, x1_ref, gfin_ref, o_ref,
                        acc_ref, p_ref, *, ni, final_norm):
    step = pl.program_id(1)

    @pl.when(step == 0)
    def _():
        acc_ref[...] = jnp.zeros_like(acc_ref)

    at = _dot(u_ref[...], h2t_ref[...])
    for ib in range(ni):
        i = step * ni + ib
        w = None
        for h in range(PEER_HEADS):
            th = theta_ref[h, pl.ds(i, 1), :]
            cf = coef_ref[h, pl.ds(i, 1), :]
            term = jnp.where(s2_ref[h] >= th, e2_ref[h] * cf, 0.0)
            w = term if w is None else w + term
        a = at[ib * PEER_NKEYS:(ib + 1) * PEER_NKEYS]
        gelu = 0.5 * a * (1.0 + lax.erf(a * (2.0 ** -0.5)))
        p_ref[ib * PEER_NKEYS:(ib + 1) * PEER_NKEYS, :] = _bf(gelu * w)
    acc_ref[...] += _dot(vt_ref[...], p_ref[...])

    @pl.when(step == pl.num_programs(1) - 1)
    def _():
        x2 = x1_ref[...] + acc_ref[...].T
        o_ref[...] = _rms(x2, gfin_ref[...]) if final_norm else x2


def _peer_experts(h2t, u, vt, s4, e2, theta, coef, x1, gfin, tm, ni, final_norm):
    n = x1.shape[0]
    et = ni * PEER_NKEYS
    per_head = pl.BlockSpec((PEER_HEADS, PEER_NKEYS, tm), lambda t, i: (0, 0, t))
    return pl.pallas_call(
        functools.partial(_peer_expert_kernel, ni=ni, final_norm=final_norm),
        grid=(n // tm, PEER_NKEYS // ni),
        in_specs=[pl.BlockSpec((D_MODEL, tm), lambda t, i: (0, t)),
                  pl.BlockSpec((et, D_MODEL), lambda t, i: (i, 0)),
                  pl.BlockSpec((D_MODEL, et), lambda t, i: (0, i)),
                  pl.BlockSpec((PEER_HEADS, None, PEER_NKEYS, tm), lambda t, i: (0, 1, 0, t)),
                  per_head, per_head, per_head,
                  pl.BlockSpec((tm, D_MODEL), lambda t, i: (t, 0)),
                  pl.BlockSpec(gfin.shape, lambda t, i: (0, 0))],
        out_specs=pl.BlockSpec((tm, D_MODEL), lambda t, i: (t, 0)),
        out_shape=jax.ShapeDtypeStruct((n, D_MODEL), jnp.float32),
        scratch_shapes=[pltpu.VMEM((D_MODEL, tm), jnp.float32), pltpu.VMEM((et, tm), jnp.bfloat16)],
        compiler_params=pltpu.CompilerParams(dimension_semantics=("parallel", "arbitrary"),
                                             vmem_limit_bytes=VMEM_LIMIT),
        name="peer_experts",
    )(h2t, u, vt, s4, e2, theta, coef, x1, gfin)


TOKEN_TILE = 256
ATTN_BLOCK = 256
REC_CHUNK = 128
SAMPLE_CHUNK = 16
SAMPLE_SEQS_PER_STEP = 8
PAGES_PER_STEP = 8
SELECT_LANES = 256
EXPERT_TOKEN_TILE = 512
EXPERT_ROWS_PER_STEP = 4


def kernel(x_prompt, x_sample, cache_kv_latent, cache_k_rope, state_retention, state_hgrn, page_table,
           w_in, q_norm, w_uq, kv_norm, w_uk, w_uv, ret_norm, hg_norm, hg_lower_bounds, w_out,
           attn_norm, ffn_norm, final_norm, peer_wq, peer_keys, peer_u, peer_v):
    batch, seq, d = x_prompt.shape
    dec_batch, td, _ = x_sample.shape
    depth = w_in.shape[0]
    n_p, n_s = batch * seq, dec_batch * td
    past_len = page_table.shape[1] * PAGE_SIZE
    f32 = jnp.float32

    x = jnp.concatenate([x_prompt.reshape(n_p, d), x_sample.reshape(n_s, d)], axis=0)
    pos = jnp.concatenate([jnp.tile(jnp.arange(seq, dtype=jnp.int32), batch),
                           jnp.tile(past_len + jnp.arange(td, dtype=jnp.int32), dec_batch)])
    tabs = _rope_tables(pos)
    lb_soft = jax.nn.softmax(hg_lower_bounds.astype(f32), axis=0)
    lbs = jnp.cumsum(lb_soft, axis=0) - lb_soft[0]
    zeros_state = jnp.zeros((batch, RET_HEADS, RET_DK, RET_DV), f32)
    nch = seq // REC_CHUNK
    row2 = lambda a: a.reshape(1, -1)

    outs = {k: [] for k in ("p_ckv", "p_kr", "p_ret", "p_hg", "s_ckv", "s_kr", "s_ret", "s_hg")}
    for l in range(depth):
        wbig, wuq, wuk, wuv = _prep_layer_weights(w_in[l], w_uq[l], w_uk[l], w_uv[l])
        ckv, kr, q, k, v, zp = _proj_in(x, row2(attn_norm[l]), wbig, row2(q_norm[l]), row2(kv_norm[l]), wuq, wuk,
                                        wuv, row2(lbs[l]), tabs, TOKEN_TILE)
        o_p = _attn_prompt(q, k, v, batch, seq, ATTN_BLOCK)
        o_s = _attn_sample(page_table, q, _bf(jnp.transpose(w_uk[l], (1, 0, 2))), _place_wuv(w_uv[l]), ckv, kr,
                           cache_kv_latent, cache_k_rope, l, n_p, dec_batch, td, PAGES_PER_STEP)
        o_mla = jnp.concatenate([o_p, _bf(o_s)], axis=0)
        zp_p = zp.reshape(-1, REC_CHUNK, zp.shape[-1])
        zp_s = jnp.pad(zp[n_p:].reshape(dec_batch, td, -1), ((0, 0), (0, SAMPLE_CHUNK - td), (0, 0)))
        gn_ret, gn_hg = row2(ret_norm[l]), row2(hg_norm[l])
        ret_p, ret_sp = _retention(zp_p, zeros_state, gn_ret, 1, nch, REC_CHUNK)
        ret_s, ret_ss = _retention(zp_s, state_retention[l], gn_ret, SAMPLE_SEQS_PER_STEP, 1, td)
        hg_p, hg_sp = _hgrn(zp_p, zeros_state, gn_hg, 1, nch)
        hg_s, hg_ss = _hgrn(zp_s, state_hgrn[l], gn_hg, SAMPLE_SEQS_PER_STEP, 1)
        join = lambda p, s: jnp.concatenate([p.reshape(n_p, REC_W), s[:, :td].reshape(n_s, REC_W)], axis=0)
        o_ret, o_hg = join(ret_p, ret_s), join(hg_p, hg_s)
        wqh, wql = _split2(peer_wq[l])
        kh, kl = _split2(peer_keys[l].reshape(2 * PEER_HEADS, PEER_NKEYS, PEER_HALF))
        x1, h2t, st = _out_proj(x, o_mla, o_ret, o_hg, _bf(w_out[l]), row2(ffn_norm[l]), wqh, wql, kh, kl,
                                TOKEN_TILE)
        theta, coef, e2 = _peer_select(st, SELECT_LANES)
        x = _peer_experts(h2t, _bf(peer_u[l]), _bf(peer_v[l].T), st.reshape(PEER_HEADS, 2, PEER_NKEYS, -1), e2,
                          theta, coef, x1, row2(final_norm), EXPERT_TOKEN_TILE, EXPERT_ROWS_PER_STEP,
                          l == depth - 1)
        outs["p_ckv"].append(ckv[:n_p].reshape(batch, seq, KV_LORA))
        outs["p_kr"].append(kr[:n_p].reshape(batch, seq, QK_ROPE))
        outs["p_ret"].append(ret_sp)
        outs["p_hg"].append(hg_sp)
        outs["s_ckv"].append(ckv[n_p:].reshape(dec_batch, td, KV_LORA))
        outs["s_kr"].append(kr[n_p:].reshape(dec_batch, td, QK_ROPE))
        outs["s_ret"].append(ret_ss)
        outs["s_hg"].append(hg_ss)
    y_prompt = x[:n_p].reshape(batch, seq, d)
    y_sample = x[n_p:].reshape(dec_batch, td, d)
    return (y_prompt, y_sample) + tuple(jnp.stack(outs[k]) for k in
                                        ("p_ckv", "p_kr", "p_ret", "p_hg", "s_ckv", "s_kr", "s_ret", "s_hg"))
```

```python
import functools
import math

import jax
import jax.numpy as jnp
from jax import lax
from jax.experimental import pallas as pl
from jax.experimental.pallas import tpu as pltpu

D_MODEL = 1024
MLA_HEADS = 8
QK_NOPE = 64
QK_ROPE = 32
V_HEAD = 64
Q_LORA = 256
KV_LORA = 128
RET_HEADS = 4
RET_DK = 64
RET_DV = 64
HG_HEADS = 4
HG_DK = 64
HG_DV = 64
PEER_HEADS = 8
PEER_NKEYS = 128
PEER_HALF = 128
PEER_TOPK = 16
ROPE_BASE = 10000.0
EPS = 1e-6
PAGE_SIZE = 128

LANES = 128
HEAD_SLOT = 128
REC_W = 256
NEG_INF = float("-inf")
VMEM_LIMIT = 56 * 1024 * 1024

ZP_RQ, ZP_RK, ZP_RV, ZP_RG, ZP_GQ, ZP_GF, ZP_GK, ZP_GI, ZP_GG = range(9)
ZP_GROUPS = 9


def _bf(x):
    return x.astype(jnp.bfloat16)


def _dot(a, b):
    return jnp.dot(a, b, preferred_element_type=jnp.float32)


def _dot_nt(a, b):
    return lax.dot_general(a, b, (((1,), (1,)), ((), ())), preferred_element_type=jnp.float32)


def _dot_tn(a, b):
    return lax.dot_general(a, b, (((0,), (0,)), ((), ())), preferred_element_type=jnp.float32)


def _rms(x, g):
    return x * lax.rsqrt(jnp.mean(x * x, axis=-1, keepdims=True) + EPS) * g


def _silu(x):
    return x * (1.0 / (1.0 + jnp.exp(-x)))


def _split3(x):
    x1 = _bf(x)
    r1 = x - x1.astype(jnp.float32)
    x2 = _bf(r1)
    x3 = _bf(r1 - x2.astype(jnp.float32))
    return x1, x2, x3


def _proj_in_kernel(x_ref, gattn_ref, wbig_ref, qn_ref, kvn_ref, wuq_ref, wuk_ref, wuv_ref, lb_ref,
                    cq_ref, sq_ref, c64_ref, s64_ref,
                    ckv_ref, kr_ref, q_ref, k_ref, v_ref, zp_ref):
    scale = (QK_NOPE + QK_ROPE) ** -0.5 * math.log2(math.e)
    h = _bf(_rms(x_ref[...], gattn_ref[...]))
    z = _dot(h, wbig_ref[...])
    o = 0
    cq = z[:, o:o + Q_LORA]; o += Q_LORA
    ckv = z[:, o:o + KV_LORA]; o += KV_LORA
    krp = z[:, o:o + HEAD_SLOT]; o += HEAD_SLOT
    krp_rot = z[:, o:o + HEAD_SLOT]; o += HEAD_SLOT
    groups = []
    for _ in range(10):
        groups.append(z[:, o:o + REC_W]); o += REC_W
    rq, rq_rot, rk, rk_rot, rv, rg, gq, gf, gi, gg = groups

    cq_t = cq_ref[...]
    sq_t = sq_ref[...]
    cqn = _bf(_rms(cq, qn_ref[...]))
    zq = _dot(cqn, wuq_ref[...])
    hw = MLA_HEADS * HEAD_SLOT
    for hd in range(MLA_HEADS):
        a = zq[:, hd * HEAD_SLOT:(hd + 1) * HEAD_SLOT]
        b = zq[:, hw + hd * HEAD_SLOT:hw + (hd + 1) * HEAD_SLOT]
        q_ref[hd] = _bf((a * cq_t + b * sq_t) * scale)
    ckvn = _rms(ckv, kvn_ref[...])
    ckv_ref[...] = ckvn
    kr_placed = krp * cq_t + krp_rot * sq_t
    kr_ref[...] = kr_placed[:, QK_NOPE:QK_NOPE + QK_ROPE]
    ckvb = _bf(ckvn)
    kn = _dot(ckvb, wuk_ref[...])
    for hd in range(MLA_HEADS):
        k_ref[hd] = _bf(kn[:, hd * HEAD_SLOT:(hd + 1) * HEAD_SLOT] + kr_placed)
    vv = _dot(ckvb, wuv_ref[...])
    lane = lax.broadcasted_iota(jnp.int32, (vv.shape[0], HEAD_SLOT), 1)
    for hd in range(MLA_HEADS):
        v_ref[hd] = _bf(jnp.where(lane < V_HEAD, vv[:, hd * HEAD_SLOT:(hd + 1) * HEAD_SLOT], 1.0))

    c64 = c64_ref[...]
    s64 = s64_ref[...]
    zp_ref[:, ZP_RQ * REC_W:(ZP_RQ + 1) * REC_W] = rq * c64 + rq_rot * s64
    zp_ref[:, ZP_RK * REC_W:(ZP_RK + 1) * REC_W] = (rk * c64 + rk_rot * s64) * (RET_DK ** -0.5)
    zp_ref[:, ZP_RV * REC_W:(ZP_RV + 1) * REC_W] = rv
    zp_ref[:, ZP_RG * REC_W:(ZP_RG + 1) * REC_W] = rg
    lb = lb_ref[...]
    sig = 1.0 / (1.0 + jnp.exp(-gf))
    zp_ref[:, ZP_GQ * REC_W:(ZP_GQ + 1) * REC_W] = _silu(gq)
    zp_ref[:, ZP_GF * REC_W:(ZP_GF + 1) * REC_W] = jnp.log(lb + (1.0 - lb) * sig)
    zp_ref[:, ZP_GK * REC_W:(ZP_GK + 1) * REC_W] = (1.0 - lb) * (1.0 / (1.0 + jnp.exp(gf)))
    zp_ref[:, ZP_GI * REC_W:(ZP_GI + 1) * REC_W] = gi
    zp_ref[:, ZP_GG * REC_W:(ZP_GG + 1) * REC_W] = gg


def _rot_cols(w, n_heads, d):
    k = w.shape[0]
    w4 = w.reshape(k, n_heads, 2, d // 2)
    return jnp.concatenate([-w4[:, :, 1:2], w4[:, :, 0:1]], axis=2).reshape(k, n_heads * d)


def _prep_layer_weights(w_in, w_uq, w_uk, w_uv):
    sizes = (Q_LORA, KV_LORA, QK_ROPE) + (REC_W,) * 8
    offs = [0]
    for s in sizes:
        offs.append(offs[-1] + s)
    cols = [w_in[:, offs[i]:offs[i + 1]] for i in range(len(sizes))]
    cq, ckv, kr, rq, rk, rv, rg, gq, gf, gi, gg = cols
    kdim = w_in.shape[0]
    zeros = lambda n: jnp.zeros((kdim, n), w_in.dtype)
    place = lambda c: jnp.concatenate([zeros(QK_NOPE), c, zeros(HEAD_SLOT - QK_NOPE - QK_ROPE)], axis=1)
    wbig = jnp.concatenate([
        cq, ckv, place(kr), place(_rot_cols(kr, 1, QK_ROPE)),
        rq, _rot_cols(rq, RET_HEADS, RET_DK), rk, _rot_cols(rk, RET_HEADS, RET_DK),
        rv, rg, gq, gf, gi, gg], axis=1)
    wq3 = w_uq.reshape(Q_LORA, MLA_HEADS, QK_NOPE + QK_ROPE)
    nope, ropep = wq3[..., :QK_NOPE], wq3[..., QK_NOPE:]
    rope_rot = jnp.concatenate([-ropep[..., QK_ROPE // 2:], ropep[..., :QK_ROPE // 2]], axis=-1)
    pad = jnp.zeros((Q_LORA, MLA_HEADS, HEAD_SLOT - QK_NOPE - QK_ROPE), w_uq.dtype)
    plain = jnp.concatenate([nope, ropep, pad], axis=-1).reshape(Q_LORA, MLA_HEADS * HEAD_SLOT)
    rotd = jnp.concatenate([jnp.zeros_like(nope), rope_rot, pad], axis=-1).reshape(Q_LORA, MLA_HEADS * HEAD_SLOT)
    wuq = jnp.concatenate([plain, rotd], axis=1)
    wuk = jnp.concatenate([w_uk, jnp.zeros((KV_LORA, MLA_HEADS, HEAD_SLOT - QK_NOPE), w_uk.dtype)],
                          axis=-1).reshape(KV_LORA, MLA_HEADS * HEAD_SLOT)
    wuv = jnp.concatenate([w_uv, jnp.zeros((KV_LORA, MLA_HEADS, HEAD_SLOT - V_HEAD), w_uv.dtype)],
                          axis=-1).reshape(KV_LORA, MLA_HEADS * HEAD_SLOT)
    return _bf(wbig), _bf(wuq), _bf(wuk), _bf(wuv)


def _rope_tables(pos):
    posf = pos.astype(jnp.float32)[:, None]

    def cs(d):
        half = d // 2
        inv = ROPE_BASE ** (-jnp.arange(half, dtype=jnp.float32) / half)
        ang = posf * inv[None, :]
        return jnp.cos(ang), jnp.sin(ang)

    n = pos.shape[0]
    c32, s32 = cs(QK_ROPE)
    tail = jnp.zeros((n, HEAD_SLOT - QK_NOPE - QK_ROPE), jnp.float32)
    cq = jnp.concatenate([jnp.ones((n, QK_NOPE), jnp.float32), c32, c32, tail], axis=1)
    sq = jnp.concatenate([jnp.zeros((n, QK_NOPE), jnp.float32), s32, s32, tail], axis=1)
    c64, s64 = cs(RET_DK)
    c64 = jnp.tile(jnp.concatenate([c64, c64], axis=1), (1, RET_HEADS))
    s64 = jnp.tile(jnp.concatenate([s64, s64], axis=1), (1, RET_HEADS))
    return cq, sq, c64, s64


def _proj_in(x, gattn, wbig, qn, kvn, wuq, wuk, wuv, lb, tabs, tm):
    n = x.shape[0]
    cq_t, sq_t, c64, s64 = tabs
    full = lambda a: pl.BlockSpec(a.shape, lambda i: (0,) * a.ndim)
    row = lambda w: pl.BlockSpec((tm, w), lambda i: (i, 0))
    hrow = lambda nh: pl.BlockSpec((nh, tm, HEAD_SLOT), lambda i: (0, i, 0))
    out_shape = (
        jax.ShapeDtypeStruct((n, KV_LORA), jnp.float32),
        jax.ShapeDtypeStruct((n, QK_ROPE), jnp.float32),
        jax.ShapeDtypeStruct((MLA_HEADS, n, HEAD_SLOT), jnp.bfloat16),
        jax.ShapeDtypeStruct((MLA_HEADS, n, HEAD_SLOT), jnp.bfloat16),
        jax.ShapeDtypeStruct((MLA_HEADS, n, HEAD_SLOT), jnp.bfloat16),
        jax.ShapeDtypeStruct((n, ZP_GROUPS * REC_W), jnp.float32),
    )
    return pl.pallas_call(
        _proj_in_kernel,
        grid=(n // tm,),
        in_specs=[row(D_MODEL), full(gattn), full(wbig), full(qn), full(kvn), full(wuq), full(wuk), full(wuv),
                  full(lb), row(HEAD_SLOT), row(HEAD_SLOT), row(REC_W), row(REC_W)],
        out_specs=(row(KV_LORA), row(QK_ROPE), hrow(MLA_HEADS), hrow(MLA_HEADS), hrow(MLA_HEADS),
                   row(ZP_GROUPS * REC_W)),
        out_shape=out_shape,
        compiler_params=pltpu.CompilerParams(dimension_semantics=("parallel",), vmem_limit_bytes=VMEM_LIMIT),
        name="proj_in",
    )(x, gattn, wbig, qn, kvn, wuq, wuk, wuv, lb, cq_t, sq_t, c64, s64)


def _online_step(s, vb, m, acc):
    m_new = jnp.maximum(m, jnp.max(s, axis=-1, keepdims=True))
    acc = jnp.exp2(m - m_new) * acc + _dot(_bf(jnp.exp2(s - m_new)), vb)
    return m_new, acc


def _attn_prompt_kernel(q_ref, k_ref, v_ref, o_ref, *, blk):
    qi = pl.program_id(2)
    qs = (q_ref[0], q_ref[1])

    def step(j, carry, masked):
        start = pl.multiple_of(j * blk, blk)
        out = []
        for hh in range(2):
            s = _dot_nt(qs[hh], k_ref[hh, pl.ds(start, blk), :])
            if masked:
                row = lax.broadcasted_iota(jnp.int32, (blk, blk), 0)
                col = lax.broadcasted_iota(jnp.int32, (blk, blk), 1)
                s = jnp.where(col <= row, s, NEG_INF)
            out.append(_online_step(s, v_ref[hh, pl.ds(start, blk), :], *carry[hh]))
        return tuple(out)

    init = (jnp.full((blk, 1), NEG_INF, jnp.float32), jnp.zeros((blk, LANES), jnp.float32))
    carry = lax.fori_loop(0, qi, lambda j, c: step(j, c, False), (init, init))
    carry = step(qi, carry, True)
    norm = [acc / pltpu.roll(acc, V_HEAD, axis=1) for _, acc in carry]
    lane = lax.broadcasted_iota(jnp.int32, (blk, LANES), 1)
    o_ref[...] = _bf(jnp.where(lane < V_HEAD, norm[0], pltpu.roll(norm[1], V_HEAD, axis=1)))


def _attn_prompt(q, k, v, batch, seq, blk):
    nq = seq // blk
    return pl.pallas_call(
        functools.partial(_attn_prompt_kernel, blk=blk),
        grid=(batch, MLA_HEADS // 2, nq),
        in_specs=[pl.BlockSpec((2, blk, HEAD_SLOT), lambda b, p, i: (p, b * nq + i, 0)),
                  pl.BlockSpec((2, seq, HEAD_SLOT), lambda b, p, i: (p, b, 0)),
                  pl.BlockSpec((2, seq, LANES), lambda b, p, i: (p, b, 0))],
        out_specs=pl.BlockSpec((blk, LANES), lambda b, p, i: (b * nq + i, p)),
        out_shape=jax.ShapeDtypeStruct((batch * seq, MLA_HEADS * V_HEAD), jnp.bfloat16),
        compiler_params=pltpu.CompilerParams(dimension_semantics=("parallel", "parallel", "arbitrary"),
                                             vmem_limit_bytes=VMEM_LIMIT),
        name="attn_prompt",
    )(q, k, v)


def _attn_sample_kernel(pt_ref, q_ref, wuk_ref, wuvp_ref, ckv_ref, kr_ref, *rest, pages_per_step, td):
    kv_refs = rest[:pages_per_step]
    krt_refs = rest[pages_per_step:2 * pages_per_step]
    o_ref = rest[2 * pages_per_step]
    qlat_ref, qrope_ref, m_ref, l_ref, acc_ref, kall_ref, krt_ref = rest[2 * pages_per_step + 1:]
    j = pl.program_id(1)
    rows = MLA_HEADS * td

    @pl.when(j == 0)
    def _():
        for hd in range(MLA_HEADS):
            qh = q_ref[hd].astype(jnp.float32)
            qlat_ref[hd * td:(hd + 1) * td, :] = _dot_nt(_bf(qh[:, :QK_NOPE]), wuk_ref[hd])
            qrope_ref[hd * td:(hd + 1) * td, :] = qh[:, QK_NOPE:QK_NOPE + QK_ROPE]
        m_ref[...] = jnp.full((rows, 1), NEG_INF, jnp.float32)
        l_ref[...] = jnp.zeros((rows, 1), jnp.float32)
        acc_ref[...] = jnp.zeros((rows, KV_LORA), jnp.float32)

    for r in range(pages_per_step):
        kall_ref[r * PAGE_SIZE:(r + 1) * PAGE_SIZE, :] = _bf(kv_refs[r][...])
        krt_ref[:, r * PAGE_SIZE:(r + 1) * PAGE_SIZE] = _bf(krt_refs[r][...])
    qlat = _bf(qlat_ref[...])
    qrope = _bf(qrope_ref[...])
    kall = kall_ref[...]
    s = _dot_nt(qlat, kall) + _dot(qrope, krt_ref[...])
    m = m_ref[...]
    m_new = jnp.maximum(m, jnp.max(s, axis=-1, keepdims=True))
    alpha = jnp.exp2(m - m_new)
    p = jnp.exp2(s - m_new)
    l = alpha * l_ref[...] + jnp.sum(p, axis=-1, keepdims=True)
    acc = alpha * acc_ref[...] + _dot(_bf(p), kall)
    m_ref[...], l_ref[...], acc_ref[...] = m_new, l, acc

    @pl.when(j == pl.num_programs(1) - 1)
    def _():
        pad = jnp.zeros((PAGE_SIZE - td, KV_LORA), jnp.float32)
        cnew = _bf(jnp.concatenate([ckv_ref[...], pad], axis=0))
        krnew = _bf(jnp.concatenate([kr_ref[...], pad[:, :QK_ROPE]], axis=0))
        sn = _dot_nt(qlat, cnew) + _dot_nt(qrope, krnew)
        row = lax.broadcasted_iota(jnp.int32, (rows, PAGE_SIZE), 0)
        col = lax.broadcasted_iota(jnp.int32, (rows, PAGE_SIZE), 1)
        sn = jnp.where(col <= row % td, sn, NEG_INF)
        m2 = jnp.maximum(m_new, jnp.max(sn, axis=-1, keepdims=True))
        a2 = jnp.exp2(m_new - m2)
        pn = jnp.exp2(sn - m2)
        l2 = a2 * l + jnp.sum(pn, axis=-1, keepdims=True)
        acc2 = a2 * acc + _dot(_bf(pn), cnew)
        olat = _bf(acc2 / l2)
        out = jnp.zeros((td, MLA_HEADS * V_HEAD), jnp.float32)
        for hd in range(MLA_HEADS):
            out = out + _dot(olat[hd * td:(hd + 1) * td, :], wuvp_ref[hd])
        o_ref[...] = out


def _attn_sample(page_table, q, wuk_h, wuv_placed, ckv, kr, cache_kv, cache_krt, layer, row0, dec_batch, td,
                 pages_per_step):
    n_pages = page_table.shape[1]
    steps = n_pages // pages_per_step
    blk0 = row0 // td
    rows = MLA_HEADS * td
    keys = pages_per_step * PAGE_SIZE

    def page_spec(shape, r):
        return pl.BlockSpec((None, None) + shape,
                            lambda b, j, pt, r=r: (layer, pt[b * n_pages + j * pages_per_step + r], 0, 0))

    in_specs = [pl.BlockSpec((MLA_HEADS, td, HEAD_SLOT), lambda b, j, pt: (0, blk0 + b, 0)),
                pl.BlockSpec(wuk_h.shape, lambda b, j, pt: (0, 0, 0)),
                pl.BlockSpec(wuv_placed.shape, lambda b, j, pt: (0, 0, 0)),
                pl.BlockSpec((td, KV_LORA), lambda b, j, pt: (blk0 + b, 0)),
                pl.BlockSpec((td, QK_ROPE), lambda b, j, pt: (blk0 + b, 0))]
    in_specs += [page_spec((PAGE_SIZE, KV_LORA), r) for r in range(pages_per_step)]
    in_specs += [page_spec((QK_ROPE, PAGE_SIZE), r) for r in range(pages_per_step)]
    grid_spec = pltpu.PrefetchScalarGridSpec(
        num_scalar_prefetch=1,
        grid=(dec_batch, steps),
        in_specs=in_specs,
        out_specs=pl.BlockSpec((td, MLA_HEADS * V_HEAD), lambda b, j, pt: (b, 0)),
        scratch_shapes=[pltpu.VMEM((rows, KV_LORA), jnp.float32), pltpu.VMEM((rows, QK_ROPE), jnp.float32),
                        pltpu.VMEM((rows, 1), jnp.float32), pltpu.VMEM((rows, 1), jnp.float32),
                        pltpu.VMEM((rows, KV_LORA), jnp.float32),
                        pltpu.VMEM((keys, KV_LORA), jnp.bfloat16), pltpu.VMEM((QK_ROPE, keys), jnp.bfloat16)])
    return pl.pallas_call(
        functools.partial(_attn_sample_kernel, pages_per_step=pages_per_step, td=td),
        grid_spec=grid_spec,
        out_shape=jax.ShapeDtypeStruct((dec_batch * td, MLA_HEADS * V_HEAD), jnp.float32),
        compiler_params=pltpu.CompilerParams(dimension_semantics=("parallel", "arbitrary"),
                                             vmem_limit_bytes=VMEM_LIMIT),
        name="attn_sample",
    )(page_table.reshape(-1), q, wuk_h, wuv_placed, ckv, kr, *([cache_kv] * pages_per_step),
      *([cache_krt] * pages_per_step))


def _place_wuv(w_uv):
    eye = jnp.eye(MLA_HEADS, dtype=w_uv.dtype)
    placed = jnp.einsum('chv,hg->hcgv', w_uv, eye)
    return _bf(placed.reshape(MLA_HEADS, KV_LORA, MLA_HEADS * V_HEAD))


def _ret_gammas():
    return [1.0 - 2.0 ** (-5.0 - h) for h in range(RET_HEADS)]


def _ret_kernel(q_ref, k_ref, v_ref, g_ref, s0_ref, dmask_ref, qdec_ref, kdec_ref, gn_ref,
                o_ref, sfin_ref, s_ref, *, nb, decay_len):
    c = pl.program_id(1)

    @pl.when(c == 0)
    def _():
        s_ref[...] = s0_ref[...]

    qdec = qdec_ref[...]
    kdec = kdec_ref[...]
    gn = gn_ref[...]
    for bi in range(nb):
        q = q_ref[bi]
        k = k_ref[bi]
        v = v_ref[bi]
        kd = k * kdec
        gate = _silu(g_ref[bi])
        for h in range(RET_HEADS):
            sl = slice(h * RET_DK, (h + 1) * RET_DK)
            qh, kh, vh = _bf(q[:, sl]), _bf(k[:, sl]), _bf(v[:, sl])
            a = _dot_nt(qh, kh) * dmask_ref[h]
            s = s_ref[bi, h]
            o = _dot(_bf(a), vh) + _dot(qh, _bf(s)) * qdec[:, sl]
            s_ref[bi, h] = s * (_ret_gammas()[h] ** decay_len) + _dot_tn(_bf(kd[:, sl]), vh)
            oc = o - jnp.mean(o, axis=-1, keepdims=True)
            y = oc * lax.rsqrt(jnp.mean(oc * oc, axis=-1, keepdims=True) + EPS)
            o_ref[bi, :, sl] = _bf(y * gn[:, sl] * gate[:, sl])

    @pl.when(c == pl.num_programs(1) - 1)
    def _():
        sfin_ref[...] = s_ref[...]


def _ret_consts(chunk, decay_len):
    lg = jnp.log(jnp.asarray(_ret_gammas(), jnp.float32))
    idx = jnp.arange(chunk, dtype=jnp.float32)
    diff = idx[:, None] - idx[None, :]
    dmask = jnp.where(diff >= 0, jnp.exp(lg[:, None, None] * jnp.maximum(diff, 0.0)), 0.0)
    qdec = jnp.exp(lg[None, :] * (idx[:, None] + 1.0))
    kdec = jnp.exp(lg[None, :] * (decay_len - 1.0 - idx[:, None]))
    rep = lambda a: jnp.repeat(a, RET_DK, axis=1)
    return dmask, rep(qdec), rep(kdec)


def _rec_specs(nb, chunk, nch, groups):
    return [pl.BlockSpec((nb, chunk, REC_W), lambda b, c, g=g: (b * nch + c, 0, g)) for g in groups]


def _retention(zp3, s0, gn, nb, nch, decay_len):
    chunk = zp3.shape[1]
    batch = s0.shape[0]
    dmask, qdec, kdec = _ret_consts(chunk, decay_len)
    full = lambda a: pl.BlockSpec(a.shape, lambda b, c: (0,) * a.ndim)
    st = pl.BlockSpec((nb, RET_HEADS, RET_DK, RET_DV), lambda b, c: (b, 0, 0, 0))
    return pl.pallas_call(
        functools.partial(_ret_kernel, nb=nb, decay_len=decay_len),
        grid=(batch // nb, nch),
        in_specs=_rec_specs(nb, chunk, nch, (ZP_RQ, ZP_RK, ZP_RV, ZP_RG)) + [st, full(dmask), full(qdec),
                                                                              full(kdec), full(gn)],
        out_specs=(pl.BlockSpec((nb, chunk, REC_W), lambda b, c: (b * nch + c, 0, 0)), st),
        out_shape=(jax.ShapeDtypeStruct((batch * nch, chunk, REC_W), jnp.bfloat16),
                   jax.ShapeDtypeStruct(s0.shape, jnp.float32)),
        scratch_shapes=[pltpu.VMEM((nb, RET_HEADS, RET_DK, RET_DV), jnp.float32)],
        compiler_params=pltpu.CompilerParams(dimension_semantics=("parallel", "arbitrary"),
                                             vmem_limit_bytes=VMEM_LIMIT),
        name="retention",
    )(zp3, zp3, zp3, zp3, s0, dmask, qdec, kdec, gn)


def _hgrn_sum_matrix(chunk):
    import numpy as np
    t = np.arange(chunk)[:, None]
    r = np.arange(chunk)[None, :]
    blocks = [r <= t, r > t]
    m = 2
    while m <= chunk:
        half = m // 2
        mid = (t // m) * m + half
        upper = (t % m) >= half
        blocks.append(upper & (r >= mid) & (r <= t))
        blocks.append(~upper & (r > t) & (r <= mid - 1))
        m *= 2
    return jnp.asarray(np.concatenate(blocks, axis=0).astype(np.float32), jnp.bfloat16)


def _hgrn_kernel(q_ref, f_ref, k_ref, v_ref, g_ref, s0_ref, msum_ref, gn_ref, o_ref, sfin_ref, st_ref, *,
                 nb, chunk):
    c = pl.program_id(1)
    nlev = chunk.bit_length() - 1

    @pl.when(c == 0)
    def _():
        for bi in range(nb):
            for h in range(HG_HEADS):
                st_ref[bi, h] = s0_ref[bi, h].T

    gn = gn_ref[...]
    rowi = lax.broadcasted_iota(jnp.int32, (chunk, REC_W), 0)
    r2 = lax.broadcasted_iota(jnp.int32, (chunk, chunk), 0)
    c2 = lax.broadcasted_iota(jnp.int32, (chunk, chunk), 1)
    for bi in range(nb):
        q = q_ref[bi]
        k = k_ref[bi]
        v = v_ref[bi]
        gate = _silu(g_ref[bi])
        f1, f2, f3 = _split3(f_ref[bi])
        sums = _dot(msum_ref[...], jnp.concatenate([f1, f2, f3], axis=1))
        sums = sums[:, :REC_W] + sums[:, REC_W:2 * REC_W] + sums[:, 2 * REC_W:]
        part = lambda i: sums[i * chunk:(i + 1) * chunk]
        b = part(0)
        qb = _bf(q * jnp.exp(b))
        ks = _bf(k * jnp.exp(part(1)))
        e_last = jnp.exp(b[chunk - 1:chunk, :])
        vb = _bf(v)
        qs, kls = [_bf(q)], [_bf(k)]
        for lev in range(1, nlev + 1):
            upper = (rowi & (1 << (lev - 1))) != 0
            qs.append(_bf(jnp.where(upper, q * jnp.exp(part(2 * lev)), 0.0)))
            kls.append(_bf(jnp.where(upper, 0.0, k * jnp.exp(part(2 * lev + 1)))))
        for h in range(HG_HEADS):
            sl = slice(h * HG_DK, (h + 1) * HG_DK)
            a = jnp.where(r2 == c2, _dot_nt(qs[0][:, sl], kls[0][:, sl]), 0.0)
            for lev in range(1, nlev + 1):
                same = (r2 >> lev) == (c2 >> lev)
                a = a + jnp.where(same, _dot_nt(qs[lev][:, sl], kls[lev][:, sl]), 0.0)
            st = st_ref[bi, h]
            o = _dot(_bf(a), vb[:, sl]) + _dot_nt(qb[:, sl], _bf(st))
            st_ref[bi, h] = st * e_last[:, sl] + _dot_tn(vb[:, sl], ks[:, sl])
            y = o * lax.rsqrt(jnp.mean(o * o, axis=-1, keepdims=True) + EPS)
            o_ref[bi, :, sl] = _bf(y * gn[:, sl] * gate[:, sl])

    @pl.when(c == pl.num_programs(1) - 1)
    def _():
        for bi in range(nb):
            for h in range(HG_HEADS):
                sfin_ref[bi, h] = st_ref[bi, h].T


def _hgrn(zp3, s0, gn, nb, nch):
    chunk = zp3.shape[1]
    batch = s0.shape[0]
    msum = _hgrn_sum_matrix(chunk)
    full = lambda a: pl.BlockSpec(a.shape, lambda b, c: (0,) * a.ndim)
    st = pl.BlockSpec((nb, HG_HEADS, HG_DK, HG_DV), lambda b, c: (b, 0, 0, 0))
    return pl.pallas_call(
        functools.partial(_hgrn_kernel, nb=nb, chunk=chunk),
        grid=(batch // nb, nch),
        in_specs=_rec_specs(nb, chunk, nch, (ZP_GQ, ZP_GF, ZP_GK, ZP_GI, ZP_GG)) + [st, full(msum), full(gn)],
        out_specs=(pl.BlockSpec((nb, chunk, REC_W), lambda b, c: (b * nch + c, 0, 0)), st),
        out_shape=(jax.ShapeDtypeStruct((batch * nch, chunk, REC_W), jnp.bfloat16),
                   jax.ShapeDtypeStruct(s0.shape, jnp.float32)),
        scratch_shapes=[pltpu.VMEM((nb, HG_HEADS, HG_DV, HG_DK), jnp.float32)],
        compiler_params=pltpu.CompilerParams(dimension_semantics=("parallel", "arbitrary"),
                                             vmem_limit_bytes=VMEM_LIMIT),
        name="hgrn2",
    )(zp3, zp3, zp3, zp3, zp3, s0, msum, gn)


def _dot3(ah, al, bh, bl, dot):
    return dot(ah, bh) + dot(al, bh) + dot(ah, bl)


def _split2(x):
    hi = _bf(x)
    return hi, _bf(x - hi.astype(jnp.float32))


def _out_proj_kernel(x_ref, omla_ref, oret_ref, ohg_ref, wout_ref, gffn_ref, wqh_ref, wql_ref, kh_ref, kl_ref,
                     x1_ref, h2t_ref, st_ref):
    mix = jnp.concatenate([omla_ref[...], oret_ref[...], ohg_ref[...]], axis=1)
    x1 = x_ref[...] + _dot(mix, wout_ref[...])
    x1_ref[...] = x1
    h2 = _rms(x1, gffn_ref[...])
    h2t_ref[...] = _bf(h2.T)
    hh, hl = _split2(h2)
    q = _dot3(hh, hl, wqh_ref[...], wql_ref[...], _dot)
    for p in range(2 * PEER_HEADS):
        qh, ql = _split2(q[:, p * PEER_HALF:(p + 1) * PEER_HALF])
        st_ref[p] = _dot3(kh_ref[p], kl_ref[p], qh, ql, _dot_nt)


def _out_proj(x, omla, oret, ohg, wout, gffn, wqh, wql, kh, kl, tm):
    n = x.shape[0]
    full = lambda a: pl.BlockSpec(a.shape, lambda i: (0,) * a.ndim)
    row = lambda w: pl.BlockSpec((tm, w), lambda i: (i, 0))
    return pl.pallas_call(
        _out_proj_kernel,
        grid=(n // tm,),
        in_specs=[row(D_MODEL), row(MLA_HEADS * V_HEAD), row(REC_W), row(REC_W), full(wout), full(gffn), full(wqh),
                  full(wql), full(kh), full(kl)],
        out_specs=(row(D_MODEL), pl.BlockSpec((D_MODEL, tm), lambda i: (0, i)),
                   pl.BlockSpec((2 * PEER_HEADS, PEER_NKEYS, tm), lambda i: (0, 0, i))),
        out_shape=(jax.ShapeDtypeStruct((n, D_MODEL), jnp.float32),
                   jax.ShapeDtypeStruct((D_MODEL, n), jnp.bfloat16),
                   jax.ShapeDtypeStruct((2 * PEER_HEADS, PEER_NKEYS, n), jnp.float32)),
        compiler_params=pltpu.CompilerParams(dimension_semantics=("parallel",), vmem_limit_bytes=VMEM_LIMIT),
        name="out_proj_peer_scores",
    )(x, omla, oret, ohg, wout, gffn, wqh, wql, kh, kl)


def _extract_top(x, count, rowf, on_value):
    for r in range(count):
        m = jnp.max(x, axis=0, keepdims=True)
        on_value(r, m)
        first = jnp.min(jnp.where(x == m, rowf, float(x.shape[0])), axis=0, keepdims=True)
        x = jnp.where(rowf == first, NEG_INF, x)
    return x


def _peer_select_kernel(s_ref, theta_ref, coef_ref, e2_ref, v1_ref, v2_ref, cand_ref):
    s1 = s_ref[0]
    s2 = s_ref[1]
    rowf = lax.broadcasted_iota(jnp.int32, s1.shape, 0).astype(jnp.float32)

    def keep(ref):
        def on_value(r, m):
            ref[r:r + 1, :] = m
        return on_value

    _extract_top(s1, PEER_TOPK + 1, rowf, keep(v1_ref))
    _extract_top(s2, PEER_TOPK + 1, rowf, keep(v2_ref))
    v2_top = v2_ref[0:PEER_TOPK, :]
    for r1 in range(PEER_TOPK):
        cand_ref[r1 * PEER_TOPK:(r1 + 1) * PEER_TOPK, :] = v1_ref[r1:r1 + 1, :] + v2_top
    cand = cand_ref[...]
    rowc = lax.broadcasted_iota(jnp.int32, cand.shape, 0).astype(jnp.float32)
    best = v1_ref[0:1, :] + v2_ref[0:1, :]
    picked = []
    rest = _extract_top(cand, PEER_TOPK, rowc, lambda r, m: picked.append(m))
    z = picked[0] * 0.0
    for m in picked:
        z = z + jnp.exp(m - best)
    nxt = jnp.max(rest, axis=0, keepdims=True)
    nxt = jnp.maximum(nxt, v1_ref[PEER_TOPK:PEER_TOPK + 1, :] + v2_ref[0:1, :])
    nxt = jnp.maximum(nxt, v1_ref[0:1, :] + v2_ref[PEER_TOPK:PEER_TOPK + 1, :])
    tau = 0.5 * (picked[-1] + nxt)
    theta_ref[...] = tau - s1
    coef_ref[...] = jnp.exp(s1 - v1_ref[0:1, :]) * (1.0 / z)
    e2_ref[...] = jnp.exp(s2 - v2_ref[0:1, :])


def _peer_select(st, lanes):
    n = st.shape[-1]
    s4 = st.reshape(PEER_HEADS, 2, PEER_NKEYS, n)
    out = pl.BlockSpec((None, PEER_NKEYS, lanes), lambda h, t: (h, 0, t))
    shape = jax.ShapeDtypeStruct((PEER_HEADS, PEER_NKEYS, n), jnp.float32)
    vrows = 8 * ((PEER_TOPK + 1 + 7) // 8)
    return pl.pallas_call(
        _peer_select_kernel,
        grid=(PEER_HEADS, n // lanes),
        in_specs=[pl.BlockSpec((None, 2, PEER_NKEYS, lanes), lambda h, t: (h, 0, 0, t))],
        out_specs=(out, out, out),
        out_shape=(shape, shape, shape),
        scratch_shapes=[pltpu.VMEM((vrows, lanes), jnp.float32), pltpu.VMEM((vrows, lanes), jnp.float32),
                        pltpu.VMEM((PEER_TOPK * PEER_TOPK, lanes), jnp.float32)],
        compiler_params=pltpu.CompilerParams(dimension_semantics=("parallel", "parallel"),
                                             vmem_limit_bytes=VMEM_LIMIT),
        name="peer_select",
    )(s4)


def _peer_expert_kernel(h2t_ref, u_ref, vt_ref, s2_ref, e2_ref, theta_ref, coef_ref, x1_ref, gfin_ref, o_ref,
                        acc_ref, p_ref, *, ni, final_norm):
    step = pl.program_id(1)

    @pl.when(step == 0)
    def _():
        acc_ref[...] = jnp.zeros_like(acc_ref)

    at = _dot(u_ref[...], h2t_ref[...])
    for ib in range(ni):
        i = step * ni + ib
        w = None
        for h in range(PEER_HEADS):
            th = theta_ref[h, pl.ds(i, 1), :]
            cf = coef_ref[h, pl.ds(i, 1), :]
            term = jnp.where(s2_ref[h] >= th, e2_ref[h] * cf, 0.0)
            w = term if w is None else w + term
        a = at[ib * PEER_NKEYS:(ib + 1) * PEER_NKEYS]
        gelu = 0.5 * a * (1.0 + lax.erf(a * (2.0 ** -0.5)))
        p_ref[ib * PEER_NKEYS:(ib + 1) * PEER_NKEYS, :] = _bf(gelu * w)
    acc_ref[...] += _dot(vt_ref[...], p_ref[...])

    @pl.when(step == pl.num_programs(1) - 1)
    def _():
        x2 = x1_ref[...] + acc_ref[...].T
        o_ref[...] = _rms(x2, gfin_ref[...]) if final_norm else x2


def _peer_experts(h2t, u, vt, s4, e2, theta, coef, x1, gfin, tm, ni, final_norm):
    n = x1.shape[0]
    et = ni * PEER_NKEYS
    per_head = pl.BlockSpec((PEER_HEADS, PEER_NKEYS, tm), lambda t, i: (0, 0, t))
    return pl.pallas_call(
        functools.partial(_peer_expert_kernel, ni=ni, final_norm=final_norm),
        grid=(n // tm, PEER_NKEYS // ni),
        in_specs=[pl.BlockSpec((D_MODEL, tm), lambda t, i: (0, t)),
                  pl.BlockSpec((et, D_MODEL), lambda t, i: (i, 0)),
                  pl.BlockSpec((D_MODEL, et), lambda t, i: (0, i)),
                  pl.BlockSpec((PEER_HEADS, None, PEER_NKEYS, tm), lambda t, i: (0, 1, 0, t)),
                  per_head, per_head, per_head,
                  pl.BlockSpec((tm, D_MODEL), lambda t, i: (t, 0)),
                  pl.BlockSpec(gfin.shape, lambda t, i: (0, 0))],
        out_specs=pl.BlockSpec((tm, D_MODEL), lambda t, i: (t, 0)),
        out_shape=jax.ShapeDtypeStruct((n, D_MODEL), jnp.float32),
        scratch_shapes=[pltpu.VMEM((D_MODEL, tm), jnp.float32), pltpu.VMEM((et, tm), jnp.bfloat16)],
        compiler_params=pltpu.CompilerParams(dimension_semantics=("parallel", "arbitrary"),
                                             vmem_limit_bytes=VMEM_LIMIT),
        name="peer_experts",
    )(h2t, u, vt, s4, e2, theta, coef, x1, gfin)


TOKEN_TILE = 256
ATTN_BLOCK = 512
REC_CHUNK = 128
SAMPLE_CHUNK = 16
SAMPLE_SEQS_PER_STEP = 8
PAGES_PER_STEP = 64
SELECT_LANES = 256
EXPERT_TOKEN_TILE = 512
EXPERT_ROWS_PER_STEP = 4


def kernel(x_prompt, x_sample, cache_kv_latent, cache_k_rope, state_retention, state_hgrn, page_table,
           w_in, q_norm, w_uq, kv_norm, w_uk, w_uv, ret_norm, hg_norm, hg_lower_bounds, w_out,
           attn_norm, ffn_norm, final_norm, peer_wq, peer_keys, peer_u, peer_v):
    batch, seq, d = x_prompt.shape
    dec_batch, td, _ = x_sample.shape
    depth = w_in.shape[0]
    n_p, n_s = batch * seq, dec_batch * td
    past_len = page_table.shape[1] * PAGE_SIZE
    f32 = jnp.float32

    x = jnp.concatenate([x_prompt.reshape(n_p, d), x_sample.reshape(n_s, d)], axis=0)
    pos = jnp.concatenate([jnp.tile(jnp.arange(seq, dtype=jnp.int32), batch),
                           jnp.tile(past_len + jnp.arange(td, dtype=jnp.int32), dec_batch)])
    tabs = _rope_tables(pos)
    cache_krt = jnp.swapaxes(cache_k_rope, 2, 3)
    pages_per_step = math.gcd(PAGES_PER_STEP, page_table.shape[1])
    lb_soft = jax.nn.softmax(hg_lower_bounds.astype(f32), axis=0)
    lbs = jnp.cumsum(lb_soft, axis=0) - lb_soft[0]
    zeros_state = jnp.zeros((batch, RET_HEADS, RET_DK, RET_DV), f32)
    nch = seq // REC_CHUNK
    row2 = lambda a: a.reshape(1, -1)

    outs = {k: [] for k in ("p_ckv", "p_kr", "p_ret", "p_hg", "s_ckv", "s_kr", "s_ret", "s_hg")}
    for l in range(depth):
        wbig, wuq, wuk, wuv = _prep_layer_weights(w_in[l], w_uq[l], w_uk[l], w_uv[l])
        ckv, kr, q, k, v, zp = _proj_in(x, row2(attn_norm[l]), wbig, row2(q_norm[l]), row2(kv_norm[l]), wuq, wuk,
                                        wuv, row2(lbs[l]), tabs, TOKEN_TILE)
        o_p = _attn_prompt(q, k, v, batch, seq, ATTN_BLOCK)
        o_s = _attn_sample(page_table, q, _bf(jnp.transpose(w_uk[l], (1, 0, 2))), _place_wuv(w_uv[l]), ckv, kr,
                           cache_kv_latent, cache_krt, l, n_p, dec_batch, td, pages_per_step)
        o_mla = jnp.concatenate([o_p, _bf(o_s)], axis=0)
        zp_p = zp.reshape(-1, REC_CHUNK, zp.shape[-1])
        zp_s = jnp.pad(zp[n_p:].reshape(dec_batch, td, -1), ((0, 0), (0, SAMPLE_CHUNK - td), (0, 0)))
        gn_ret, gn_hg = row2(ret_norm[l]), row2(hg_norm[l])
        ret_p, ret_sp = _retention(zp_p, zeros_state, gn_ret, 1, nch, REC_CHUNK)
        ret_s, ret_ss = _retention(zp_s, state_retention[l], gn_ret, SAMPLE_SEQS_PER_STEP, 1, td)
        hg_p, hg_sp = _hgrn(zp_p, zeros_state, gn_hg, 1, nch)
        hg_s, hg_ss = _hgrn(zp_s, state_hgrn[l], gn_hg, SAMPLE_SEQS_PER_STEP, 1)
        join = lambda p, s: jnp.concatenate([p.reshape(n_p, REC_W), s[:, :td].reshape(n_s, REC_W)], axis=0)
        o_ret, o_hg = join(ret_p, ret_s), join(hg_p, hg_s)
        wqh, wql = _split2(peer_wq[l])
        kh, kl = _split2(peer_keys[l].reshape(2 * PEER_HEADS, PEER_NKEYS, PEER_HALF))
        x1, h2t, st = _out_proj(x, o_mla, o_ret, o_hg, _bf(w_out[l]), row2(ffn_norm[l]), wqh, wql, kh, kl,
                                TOKEN_TILE)
        theta, coef, e2 = _peer_select(st, SELECT_LANES)
        x = _peer_experts(h2t, _bf(peer_u[l]), _bf(peer_v[l].T), st.reshape(PEER_HEADS, 2, PEER_NKEYS, -1), e2,
                          theta, coef, x1, row2(final_norm), EXPERT_TOKEN_TILE, EXPERT_ROWS_PER_STEP,
                          l == depth - 1)
        outs["p_ckv"].append(ckv[:n_p].reshape(batch, seq, KV_LORA))
        outs["p_kr"].append(kr[:n_p].reshape(batch, seq, QK_ROPE))
        outs["p_ret"].append(ret_sp)
        outs["p_hg"].append(hg_sp)
        outs["s_ckv"].append(ckv[n_p:].reshape(dec_batch, td, KV_LORA))
        outs["s_kr"].append(kr[n_p:].reshape(dec_batch, td, QK_ROPE))
        outs["s_ret"].append(ret_ss)
        outs["s_hg"].append(hg_ss)
    y_prompt = x[:n_p].reshape(batch, seq, d)
    y_sample = x[n_p:].reshape(dec_batch, td, d)
    return (y_prompt, y_sample) + tuple(jnp.stack(outs[k]) for k in
                                        ("p_ckv", "p_kr", "p_ret", "p_hg", "s_ckv", "s_kr", "s_ret", "s_hg"))
```

```python
import functools
import math

import jax
import jax.numpy as jnp
from jax import lax
from jax.experimental import pallas as pl
from jax.experimental.pallas import tpu as pltpu

D_MODEL = 1024
MLA_HEADS = 8
QK_NOPE = 64
QK_ROPE = 32
V_HEAD = 64
Q_LORA = 256
KV_LORA = 128
RET_HEADS = 4
RET_DK = 64
RET_DV = 64
HG_HEADS = 4
HG_DK = 64
HG_DV = 64
PEER_HEADS = 8
PEER_NKEYS = 128
PEER_HALF = 128
PEER_TOPK = 16
ROPE_BASE = 10000.0
EPS = 1e-6
PAGE_SIZE = 128

LANES = 128
HEAD_SLOT = 128
REC_W = 256
NEG_INF = float("-inf")
VMEM_LIMIT = 56 * 1024 * 1024

ZP_RQ, ZP_RK, ZP_RV, ZP_RG, ZP_GQ, ZP_GF, ZP_GK, ZP_GI, ZP_GG = range(9)
ZP_GROUPS = 9


def _bf(x):
    return x.astype(jnp.bfloat16)


def _dot(a, b):
    return jnp.dot(a, b, preferred_element_type=jnp.float32)


def _dot_nt(a, b):
    return lax.dot_general(a, b, (((1,), (1,)), ((), ())), preferred_element_type=jnp.float32)


def _dot_tn(a, b):
    return lax.dot_general(a, b, (((0,), (0,)), ((), ())), preferred_element_type=jnp.float32)


def _rms(x, g):
    return x * lax.rsqrt(jnp.mean(x * x, axis=-1, keepdims=True) + EPS) * g


def _silu(x):
    return x * (1.0 / (1.0 + jnp.exp(-x)))


def _split3(x):
    x1 = _bf(x)
    r1 = x - x1.astype(jnp.float32)
    x2 = _bf(r1)
    x3 = _bf(r1 - x2.astype(jnp.float32))
    return x1, x2, x3


def _proj_in_kernel(x_ref, gattn_ref, wbig_ref, qn_ref, kvn_ref, wuq_ref, wuk_ref, wuv_ref, lb_ref,
                    cq_ref, sq_ref, c64_ref, s64_ref,
                    ckv_ref, kr_ref, q_ref, k_ref, v_ref, zp_ref):
    scale = (QK_NOPE + QK_ROPE) ** -0.5 * math.log2(math.e)
    h = _bf(_rms(x_ref[...], gattn_ref[...]))
    z = _dot(h, wbig_ref[...])
    o = 0
    cq = z[:, o:o + Q_LORA]; o += Q_LORA
    ckv = z[:, o:o + KV_LORA]; o += KV_LORA
    krp = z[:, o:o + HEAD_SLOT]; o += HEAD_SLOT
    krp_rot = z[:, o:o + HEAD_SLOT]; o += HEAD_SLOT
    groups = []
    for _ in range(10):
        groups.append(z[:, o:o + REC_W]); o += REC_W
    rq, rq_rot, rk, rk_rot, rv, rg, gq, gf, gi, gg = groups

    cq_t = cq_ref[...]
    sq_t = sq_ref[...]
    cqn = _bf(_rms(cq, qn_ref[...]))
    zq = _dot(cqn, wuq_ref[...])
    hw = MLA_HEADS * HEAD_SLOT
    for hd in range(MLA_HEADS):
        a = zq[:, hd * HEAD_SLOT:(hd + 1) * HEAD_SLOT]
        b = zq[:, hw + hd * HEAD_SLOT:hw + (hd + 1) * HEAD_SLOT]
        q_ref[hd] = _bf((a * cq_t + b * sq_t) * scale)
    ckvn = _rms(ckv, kvn_ref[...])
    ckv_ref[...] = ckvn
    kr_placed = krp * cq_t + krp_rot * sq_t
    kr_ref[...] = kr_placed[:, QK_NOPE:QK_NOPE + QK_ROPE]
    ckvb = _bf(ckvn)
    kn = _dot(ckvb, wuk_ref[...])
    for hd in range(MLA_HEADS):
        k_ref[hd] = _bf(kn[:, hd * HEAD_SLOT:(hd + 1) * HEAD_SLOT] + kr_placed)
    vv = _dot(ckvb, wuv_ref[...])
    lane = lax.broadcasted_iota(jnp.int32, (vv.shape[0], HEAD_SLOT), 1)
    for hd in range(MLA_HEADS):
        v_ref[hd] = _bf(jnp.where(lane < V_HEAD, vv[:, hd * HEAD_SLOT:(hd + 1) * HEAD_SLOT], 1.0))

    c64 = c64_ref[...]
    s64 = s64_ref[...]
    zp_ref[:, ZP_RQ * REC_W:(ZP_RQ + 1) * REC_W] = rq * c64 + rq_rot * s64
    zp_ref[:, ZP_RK * REC_W:(ZP_RK + 1) * REC_W] = (rk * c64 + rk_rot * s64) * (RET_DK ** -0.5)
    zp_ref[:, ZP_RV * REC_W:(ZP_RV + 1) * REC_W] = rv
    zp_ref[:, ZP_RG * REC_W:(ZP_RG + 1) * REC_W] = rg
    lb = lb_ref[...]
    sig = 1.0 / (1.0 + jnp.exp(-gf))
    zp_ref[:, ZP_GQ * REC_W:(ZP_GQ + 1) * REC_W] = _silu(gq)
    zp_ref[:, ZP_GF * REC_W:(ZP_GF + 1) * REC_W] = jnp.log(lb + (1.0 - lb) * sig)
    zp_ref[:, ZP_GK * REC_W:(ZP_GK + 1) * REC_W] = (1.0 - lb) * (1.0 / (1.0 + jnp.exp(gf)))
    zp_ref[:, ZP_GI * REC_W:(ZP_GI + 1) * REC_W] = gi
    zp_ref[:, ZP_GG * REC_W:(ZP_GG + 1) * REC_W] = gg


def _rot_cols(w, n_heads, d):
    k = w.shape[0]
    w4 = w.reshape(k, n_heads, 2, d // 2)
    return jnp.concatenate([-w4[:, :, 1:2], w4[:, :, 0:1]], axis=2).reshape(k, n_heads * d)


def _prep_layer_weights(w_in, w_uq, w_uk, w_uv):
    sizes = (Q_LORA, KV_LORA, QK_ROPE) + (REC_W,) * 8
    offs = [0]
    for s in sizes:
        offs.append(offs[-1] + s)
    cols = [w_in[:, offs[i]:offs[i + 1]] for i in range(len(sizes))]
    cq, ckv, kr, rq, rk, rv, rg, gq, gf, gi, gg = cols
    kdim = w_in.shape[0]
    zeros = lambda n: jnp.zeros((kdim, n), w_in.dtype)
    place = lambda c: jnp.concatenate([zeros(QK_NOPE), c, zeros(HEAD_SLOT - QK_NOPE - QK_ROPE)], axis=1)
    wbig = jnp.concatenate([
        cq, ckv, place(kr), place(_rot_cols(kr, 1, QK_ROPE)),
        rq, _rot_cols(rq, RET_HEADS, RET_DK), rk, _rot_cols(rk, RET_HEADS, RET_DK),
        rv, rg, gq, gf, gi, gg], axis=1)
    wq3 = w_uq.reshape(Q_LORA, MLA_HEADS, QK_NOPE + QK_ROPE)
    nope, ropep = wq3[..., :QK_NOPE], wq3[..., QK_NOPE:]
    rope_rot = jnp.concatenate([-ropep[..., QK_ROPE // 2:], ropep[..., :QK_ROPE // 2]], axis=-1)
    pad = jnp.zeros((Q_LORA, MLA_HEADS, HEAD_SLOT - QK_NOPE - QK_ROPE), w_uq.dtype)
    plain = jnp.concatenate([nope, ropep, pad], axis=-1).reshape(Q_LORA, MLA_HEADS * HEAD_SLOT)
    rotd = jnp.concatenate([jnp.zeros_like(nope), rope_rot, pad], axis=-1).reshape(Q_LORA, MLA_HEADS * HEAD_SLOT)
    wuq = jnp.concatenate([plain, rotd], axis=1)
    wuk = jnp.concatenate([w_uk, jnp.zeros((KV_LORA, MLA_HEADS, HEAD_SLOT - QK_NOPE), w_uk.dtype)],
                          axis=-1).reshape(KV_LORA, MLA_HEADS * HEAD_SLOT)
    wuv = jnp.concatenate([w_uv, jnp.zeros((KV_LORA, MLA_HEADS, HEAD_SLOT - V_HEAD), w_uv.dtype)],
                          axis=-1).reshape(KV_LORA, MLA_HEADS * HEAD_SLOT)
    return _bf(wbig), _bf(wuq), _bf(wuk), _bf(wuv)


def _rope_tables(pos):
    posf = pos.astype(jnp.float32)[:, None]

    def cs(d):
        half = d // 2
        inv = ROPE_BASE ** (-jnp.arange(half, dtype=jnp.float32) / half)
        ang = posf * inv[None, :]
        return jnp.cos(ang), jnp.sin(ang)

    n = pos.shape[0]
    c32, s32 = cs(QK_ROPE)
    tail = jnp.zeros((n, HEAD_SLOT - QK_NOPE - QK_ROPE), jnp.float32)
    cq = jnp.concatenate([jnp.ones((n, QK_NOPE), jnp.float32), c32, c32, tail], axis=1)
    sq = jnp.concatenate([jnp.zeros((n, QK_NOPE), jnp.float32), s32, s32, tail], axis=1)
    c64, s64 = cs(RET_DK)
    c64 = jnp.tile(jnp.concatenate([c64, c64], axis=1), (1, RET_HEADS))
    s64 = jnp.tile(jnp.concatenate([s64, s64], axis=1), (1, RET_HEADS))
    return cq, sq, c64, s64


def _proj_in(x, gattn, wbig, qn, kvn, wuq, wuk, wuv, lb, tabs, tm):
    n = x.shape[0]
    cq_t, sq_t, c64, s64 = tabs
    full = lambda a: pl.BlockSpec(a.shape, lambda i: (0,) * a.ndim)
    row = lambda w: pl.BlockSpec((tm, w), lambda i: (i, 0))
    hrow = lambda nh: pl.BlockSpec((nh, tm, HEAD_SLOT), lambda i: (0, i, 0))
    out_shape = (
        jax.ShapeDtypeStruct((n, KV_LORA), jnp.float32),
        jax.ShapeDtypeStruct((n, QK_ROPE), jnp.float32),
        jax.ShapeDtypeStruct((MLA_HEADS, n, HEAD_SLOT), jnp.bfloat16),
        jax.ShapeDtypeStruct((MLA_HEADS, n, HEAD_SLOT), jnp.bfloat16),
        jax.ShapeDtypeStruct((MLA_HEADS, n, HEAD_SLOT), jnp.bfloat16),
        jax.ShapeDtypeStruct((n, ZP_GROUPS * REC_W), jnp.float32),
    )
    return pl.pallas_call(
        _proj_in_kernel,
        grid=(n // tm,),
        in_specs=[row(D_MODEL), full(gattn), full(wbig), full(qn), full(kvn), full(wuq), full(wuk), full(wuv),
                  full(lb), row(HEAD_SLOT), row(HEAD_SLOT), row(REC_W), row(REC_W)],
        out_specs=(row(KV_LORA), row(QK_ROPE), hrow(MLA_HEADS), hrow(MLA_HEADS), hrow(MLA_HEADS),
                   row(ZP_GROUPS * REC_W)),
        out_shape=out_shape,
        compiler_params=pltpu.CompilerParams(dimension_semantics=("parallel",), vmem_limit_bytes=VMEM_LIMIT),
        name="proj_in",
    )(x, gattn, wbig, qn, kvn, wuq, wuk, wuv, lb, cq_t, sq_t, c64, s64)


def _online_step(s, vb, m, acc):
    m_new = jnp.maximum(m, jnp.max(s, axis=-1, keepdims=True))
    acc = jnp.exp2(m - m_new) * acc + _dot(_bf(jnp.exp2(s - m_new)), vb)
    return m_new, acc


def _attn_prompt_kernel(q_ref, k_ref, v_ref, o_ref, *, blk):
    qi = pl.program_id(2)
    qs = (q_ref[0], q_ref[1])

    def step(j, carry, masked):
        start = pl.multiple_of(j * blk, blk)
        out = []
        for hh in range(2):
            s = _dot_nt(qs[hh], k_ref[hh, pl.ds(start, blk), :])
            if masked:
                row = lax.broadcasted_iota(jnp.int32, (blk, blk), 0)
                col = lax.broadcasted_iota(jnp.int32, (blk, blk), 1)
                s = jnp.where(col <= row, s, NEG_INF)
            out.append(_online_step(s, v_ref[hh, pl.ds(start, blk), :], *carry[hh]))
        return tuple(out)

    init = (jnp.full((blk, 1), NEG_INF, jnp.float32), jnp.zeros((blk, LANES), jnp.float32))
    carry = lax.fori_loop(0, qi, lambda j, c: step(j, c, False), (init, init))
    carry = step(qi, carry, True)
    norm = [acc / pltpu.roll(acc, V_HEAD, axis=1) for _, acc in carry]
    lane = lax.broadcasted_iota(jnp.int32, (blk, LANES), 1)
    o_ref[...] = _bf(jnp.where(lane < V_HEAD, norm[0], pltpu.roll(norm[1], V_HEAD, axis=1)))


def _attn_prompt(q, k, v, batch, seq, blk):
    nq = seq // blk
    return pl.pallas_call(
        functools.partial(_attn_prompt_kernel, blk=blk),
        grid=(batch, MLA_HEADS // 2, nq),
        in_specs=[pl.BlockSpec((2, blk, HEAD_SLOT), lambda b, p, i: (p, b * nq + i, 0)),
                  pl.BlockSpec((2, seq, HEAD_SLOT), lambda b, p, i: (p, b, 0)),
                  pl.BlockSpec((2, seq, LANES), lambda b, p, i: (p, b, 0))],
        out_specs=pl.BlockSpec((blk, LANES), lambda b, p, i: (b * nq + i, p)),
        out_shape=jax.ShapeDtypeStruct((batch * seq, MLA_HEADS * V_HEAD), jnp.bfloat16),
        compiler_params=pltpu.CompilerParams(dimension_semantics=("parallel", "parallel", "arbitrary"),
                                             vmem_limit_bytes=VMEM_LIMIT),
        name="attn_prompt",
    )(q, k, v)


def _attn_sample_kernel(pt_ref, q_ref, wuk_ref, wuvp_ref, ckv_ref, kr_ref, *rest, pages_per_step, td):
    kv_refs = rest[:pages_per_step]
    krt_refs = rest[pages_per_step:2 * pages_per_step]
    o_ref = rest[2 * pages_per_step]
    qlat_ref, qrope_ref, m_ref, l_ref, acc_ref, kall_ref, krt_ref = rest[2 * pages_per_step + 1:]
    j = pl.program_id(1)
    rows = MLA_HEADS * td

    @pl.when(j == 0)
    def _():
        for hd in range(MLA_HEADS):
            qh = q_ref[hd].astype(jnp.float32)
            qlat_ref[hd * td:(hd + 1) * td, :] = _dot_nt(_bf(qh[:, :QK_NOPE]), wuk_ref[hd])
            qrope_ref[hd * td:(hd + 1) * td, :] = qh[:, QK_NOPE:QK_NOPE + QK_ROPE]
        m_ref[...] = jnp.full((rows, 1), NEG_INF, jnp.float32)
        l_ref[...] = jnp.zeros((rows, 1), jnp.float32)
        acc_ref[...] = jnp.zeros((rows, KV_LORA), jnp.float32)

    for r in range(pages_per_step):
        kall_ref[r * PAGE_SIZE:(r + 1) * PAGE_SIZE, :] = _bf(kv_refs[r][...])
        krt_ref[:, r * PAGE_SIZE:(r + 1) * PAGE_SIZE] = _bf(krt_refs[r][...])
    qlat = _bf(qlat_ref[...])
    qrope = _bf(qrope_ref[...])
    kall = kall_ref[...]
    s = _dot_nt(qlat, kall) + _dot(qrope, krt_ref[...])
    m = m_ref[...]
    m_new = jnp.maximum(m, jnp.max(s, axis=-1, keepdims=True))
    alpha = jnp.exp2(m - m_new)
    p = jnp.exp2(s - m_new)
    l = alpha * l_ref[...] + jnp.sum(p, axis=-1, keepdims=True)
    acc = alpha * acc_ref[...] + _dot(_bf(p), kall)
    m_ref[...], l_ref[...], acc_ref[...] = m_new, l, acc

    @pl.when(j == pl.num_programs(1) - 1)
    def _():
        pad = jnp.zeros((PAGE_SIZE - td, KV_LORA), jnp.float32)
        cnew = _bf(jnp.concatenate([ckv_ref[...], pad], axis=0))
        krnew = _bf(jnp.concatenate([kr_ref[...], pad[:, :QK_ROPE]], axis=0))
        sn = _dot_nt(qlat, cnew) + _dot_nt(qrope, krnew)
        row = lax.broadcasted_iota(jnp.int32, (rows, PAGE_SIZE), 0)
        col = lax.broadcasted_iota(jnp.int32, (rows, PAGE_SIZE), 1)
        sn = jnp.where(col <= row % td, sn, NEG_INF)
        m2 = jnp.maximum(m_new, jnp.max(sn, axis=-1, keepdims=True))
        a2 = jnp.exp2(m_new - m2)
        pn = jnp.exp2(sn - m2)
        l2 = a2 * l + jnp.sum(pn, axis=-1, keepdims=True)
        acc2 = a2 * acc + _dot(_bf(pn), cnew)
        olat = _bf(acc2 / l2)
        out = jnp.zeros((td, MLA_HEADS * V_HEAD), jnp.float32)
        for hd in range(MLA_HEADS):
            out = out + _dot(olat[hd * td:(hd + 1) * td, :], wuvp_ref[hd])
        o_ref[...] = out


def _attn_sample(page_table, q, wuk_h, wuv_placed, ckv, kr, cache_kv, cache_krt, layer, row0, dec_batch, td,
                 pages_per_step):
    n_pages = page_table.shape[1]
    steps = n_pages // pages_per_step
    blk0 = row0 // td
    rows = MLA_HEADS * td
    keys = pages_per_step * PAGE_SIZE

    def page_spec(shape, r):
        return pl.BlockSpec((None, None) + shape,
                            lambda b, j, pt, r=r: (layer, pt[b * n_pages + j * pages_per_step + r], 0, 0))

    in_specs = [pl.BlockSpec((MLA_HEADS, td, HEAD_SLOT), lambda b, j, pt: (0, blk0 + b, 0)),
                pl.BlockSpec(wuk_h.shape, lambda b, j, pt: (0, 0, 0)),
                pl.BlockSpec(wuv_placed.shape, lambda b, j, pt: (0, 0, 0)),
                pl.BlockSpec((td, KV_LORA), lambda b, j, pt: (blk0 + b, 0)),
                pl.BlockSpec((td, QK_ROPE), lambda b, j, pt: (blk0 + b, 0))]
    in_specs += [page_spec((PAGE_SIZE, KV_LORA), r) for r in range(pages_per_step)]
    in_specs += [page_spec((QK_ROPE, PAGE_SIZE), r) for r in range(pages_per_step)]
    grid_spec = pltpu.PrefetchScalarGridSpec(
        num_scalar_prefetch=1,
        grid=(dec_batch, steps),
        in_specs=in_specs,
        out_specs=pl.BlockSpec((td, MLA_HEADS * V_HEAD), lambda b, j, pt: (b, 0)),
        scratch_shapes=[pltpu.VMEM((rows, KV_LORA), jnp.float32), pltpu.VMEM((rows, QK_ROPE), jnp.float32),
                        pltpu.VMEM((rows, 1), jnp.float32), pltpu.VMEM((rows, 1), jnp.float32),
                        pltpu.VMEM((rows, KV_LORA), jnp.float32),
                        pltpu.VMEM((keys, KV_LORA), jnp.bfloat16), pltpu.VMEM((QK_ROPE, keys), jnp.bfloat16)])
    return pl.pallas_call(
        functools.partial(_attn_sample_kernel, pages_per_step=pages_per_step, td=td),
        grid_spec=grid_spec,
        out_shape=jax.ShapeDtypeStruct((dec_batch * td, MLA_HEADS * V_HEAD), jnp.float32),
        compiler_params=pltpu.CompilerParams(dimension_semantics=("parallel", "arbitrary"),
                                             vmem_limit_bytes=VMEM_LIMIT),
        name="attn_sample",
    )(page_table.reshape(-1), q, wuk_h, wuv_placed, ckv, kr, *([cache_kv] * pages_per_step),
      *([cache_krt] * pages_per_step))


def _place_wuv(w_uv):
    eye = jnp.eye(MLA_HEADS, dtype=w_uv.dtype)
    placed = jnp.einsum('chv,hg->hcgv', w_uv, eye)
    return _bf(placed.reshape(MLA_HEADS, KV_LORA, MLA_HEADS * V_HEAD))


def _ret_gammas():
    return [1.0 - 2.0 ** (-5.0 - h) for h in range(RET_HEADS)]


def _ret_kernel(q_ref, k_ref, v_ref, g_ref, s0_ref, dmask_ref, qdec_ref, kdec_ref, gn_ref,
                o_ref, sfin_ref, s_ref, *, nb, decay_len):
    c = pl.program_id(1)

    @pl.when(c == 0)
    def _():
        s_ref[...] = s0_ref[...]

    qdec = qdec_ref[...]
    kdec = kdec_ref[...]
    gn = gn_ref[...]
    for bi in range(nb):
        q = q_ref[bi]
        k = k_ref[bi]
        v = v_ref[bi]
        kd = k * kdec
        gate = _silu(g_ref[bi])
        for h in range(RET_HEADS):
            sl = slice(h * RET_DK, (h + 1) * RET_DK)
            qh, kh, vh = _bf(q[:, sl]), _bf(k[:, sl]), _bf(v[:, sl])
            a = _dot_nt(qh, kh) * dmask_ref[h]
            s = s_ref[bi, h]
            o = _dot(_bf(a), vh) + _dot(qh, _bf(s)) * qdec[:, sl]
            s_ref[bi, h] = s * (_ret_gammas()[h] ** decay_len) + _dot_tn(_bf(kd[:, sl]), vh)
            oc = o - jnp.mean(o, axis=-1, keepdims=True)
            y = oc * lax.rsqrt(jnp.mean(oc * oc, axis=-1, keepdims=True) + EPS)
            o_ref[bi, :, sl] = _bf(y * gn[:, sl] * gate[:, sl])

    @pl.when(c == pl.num_programs(1) - 1)
    def _():
        sfin_ref[...] = s_ref[...]


def _ret_consts(chunk, decay_len):
    lg = jnp.log(jnp.asarray(_ret_gammas(), jnp.float32))
    idx = jnp.arange(chunk, dtype=jnp.float32)
    diff = idx[:, None] - idx[None, :]
    dmask = jnp.where(diff >= 0, jnp.exp(lg[:, None, None] * jnp.maximum(diff, 0.0)), 0.0)
    qdec = jnp.exp(lg[None, :] * (idx[:, None] + 1.0))
    kdec = jnp.exp(lg[None, :] * (decay_len - 1.0 - idx[:, None]))
    rep = lambda a: jnp.repeat(a, RET_DK, axis=1)
    return dmask, rep(qdec), rep(kdec)


def _rec_specs(nb, chunk, nch, groups):
    return [pl.BlockSpec((nb, chunk, REC_W), lambda b, c, g=g: (b * nch + c, 0, g)) for g in groups]


def _retention(zp3, s0, gn, nb, nch, decay_len):
    chunk = zp3.shape[1]
    batch = s0.shape[0]
    dmask, qdec, kdec = _ret_consts(chunk, decay_len)
    full = lambda a: pl.BlockSpec(a.shape, lambda b, c: (0,) * a.ndim)
    st = pl.BlockSpec((nb, RET_HEADS, RET_DK, RET_DV), lambda b, c: (b, 0, 0, 0))
    return pl.pallas_call(
        functools.partial(_ret_kernel, nb=nb, decay_len=decay_len),
        grid=(batch // nb, nch),
        in_specs=_rec_specs(nb, chunk, nch, (ZP_RQ, ZP_RK, ZP_RV, ZP_RG)) + [st, full(dmask), full(qdec),
                                                                              full(kdec), full(gn)],
        out_specs=(pl.BlockSpec((nb, chunk, REC_W), lambda b, c: (b * nch + c, 0, 0)), st),
        out_shape=(jax.ShapeDtypeStruct((batch * nch, chunk, REC_W), jnp.bfloat16),
                   jax.ShapeDtypeStruct(s0.shape, jnp.float32)),
        scratch_shapes=[pltpu.VMEM((nb, RET_HEADS, RET_DK, RET_DV), jnp.float32)],
        compiler_params=pltpu.CompilerParams(dimension_semantics=("parallel", "arbitrary"),
                                             vmem_limit_bytes=VMEM_LIMIT),
        name="retention",
    )(zp3, zp3, zp3, zp3, s0, dmask, qdec, kdec, gn)


def _hgrn_sum_matrix(chunk):
    import numpy as np
    t = np.arange(chunk)[:, None]
    r = np.arange(chunk)[None, :]
    blocks = [r <= t, r > t]
    m = 2
    while m <= chunk:
        half = m // 2
        mid = (t // m) * m + half
        upper = (t % m) >= half
        blocks.append(upper & (r >= mid) & (r <= t))
        blocks.append(~upper & (r > t) & (r <= mid - 1))
        m *= 2
    return jnp.asarray(np.concatenate(blocks, axis=0).astype(np.float32), jnp.bfloat16)


def _hgrn_kernel(q_ref, f_ref, k_ref, v_ref, g_ref, s0_ref, msum_ref, gn_ref, o_ref, sfin_ref, st_ref, *,
                 nb, chunk):
    c = pl.program_id(1)
    nlev = chunk.bit_length() - 1

    @pl.when(c == 0)
    def _():
        for bi in range(nb):
            for h in range(HG_HEADS):
                st_ref[bi, h] = s0_ref[bi, h].T

    gn = gn_ref[...]
    rowi = lax.broadcasted_iota(jnp.int32, (chunk, REC_W), 0)
    r2 = lax.broadcasted_iota(jnp.int32, (chunk, chunk), 0)
    c2 = lax.broadcasted_iota(jnp.int32, (chunk, chunk), 1)
    for bi in range(nb):
        q = q_ref[bi]
        k = k_ref[bi]
        v = v_ref[bi]
        gate = _silu(g_ref[bi])
        f1, f2, f3 = _split3(f_ref[bi])
        sums = _dot(msum_ref[...], jnp.concatenate([f1, f2, f3], axis=1))
        sums = sums[:, :REC_W] + sums[:, REC_W:2 * REC_W] + sums[:, 2 * REC_W:]
        part = lambda i: sums[i * chunk:(i + 1) * chunk]
        b = part(0)
        qb = _bf(q * jnp.exp(b))
        ks = _bf(k * jnp.exp(part(1)))
        e_last = jnp.exp(b[chunk - 1:chunk, :])
        vb = _bf(v)
        qs, kls = [_bf(q)], [_bf(k)]
        for lev in range(1, nlev + 1):
            upper = (rowi & (1 << (lev - 1))) != 0
            qs.append(_bf(jnp.where(upper, q * jnp.exp(part(2 * lev)), 0.0)))
            kls.append(_bf(jnp.where(upper, 0.0, k * jnp.exp(part(2 * lev + 1)))))
        for h in range(HG_HEADS):
            sl = slice(h * HG_DK, (h + 1) * HG_DK)
            a = jnp.where(r2 == c2, _dot_nt(qs[0][:, sl], kls[0][:, sl]), 0.0)
            for lev in range(1, nlev + 1):
                same = (r2 >> lev) == (c2 >> lev)
                a = a + jnp.where(same, _dot_nt(qs[lev][:, sl], kls[lev][:, sl]), 0.0)
            st = st_ref[bi, h]
            o = _dot(_bf(a), vb[:, sl]) + _dot_nt(qb[:, sl], _bf(st))
            st_ref[bi, h] = st * e_last[:, sl] + _dot_tn(vb[:, sl], ks[:, sl])
            y = o * lax.rsqrt(jnp.mean(o * o, axis=-1, keepdims=True) + EPS)
            o_ref[bi, :, sl] = _bf(y * gn[:, sl] * gate[:, sl])

    @pl.when(c == pl.num_programs(1) - 1)
    def _():
        for bi in range(nb):
            for h in range(HG_HEADS):
                sfin_ref[bi, h] = st_ref[bi, h].T


def _hgrn(zp3, s0, gn, nb, nch):
    chunk = zp3.shape[1]
    batch = s0.shape[0]
    msum = _hgrn_sum_matrix(chunk)
    full = lambda a: pl.BlockSpec(a.shape, lambda b, c: (0,) * a.ndim)
    st = pl.BlockSpec((nb, HG_HEADS, HG_DK, HG_DV), lambda b, c: (b, 0, 0, 0))
    return pl.pallas_call(
        functools.partial(_hgrn_kernel, nb=nb, chunk=chunk),
        grid=(batch // nb, nch),
        in_specs=_rec_specs(nb, chunk, nch, (ZP_GQ, ZP_GF, ZP_GK, ZP_GI, ZP_GG)) + [st, full(msum), full(gn)],
        out_specs=(pl.BlockSpec((nb, chunk, REC_W), lambda b, c: (b * nch + c, 0, 0)), st),
        out_shape=(jax.ShapeDtypeStruct((batch * nch, chunk, REC_W), jnp.bfloat16),
                   jax.ShapeDtypeStruct(s0.shape, jnp.float32)),
        scratch_shapes=[pltpu.VMEM((nb, HG_HEADS, HG_DV, HG_DK), jnp.float32)],
        compiler_params=pltpu.CompilerParams(dimension_semantics=("parallel", "arbitrary"),
                                             vmem_limit_bytes=VMEM_LIMIT),
        name="hgrn2",
    )(zp3, zp3, zp3, zp3, zp3, s0, msum, gn)


def _dot3(ah, al, bh, bl, dot):
    return dot(ah, bh) + dot(al, bh) + dot(ah, bl)


def _split2(x):
    hi = _bf(x)
    return hi, _bf(x - hi.astype(jnp.float32))


def _out_proj_kernel(x_ref, omla_ref, oret_ref, ohg_ref, wout_ref, gffn_ref, wqh_ref, wql_ref, kh_ref, kl_ref,
                     x1_ref, h2t_ref, st_ref):
    mix = jnp.concatenate([omla_ref[...], oret_ref[...], ohg_ref[...]], axis=1)
    x1 = x_ref[...] + _dot(mix, wout_ref[...])
    x1_ref[...] = x1
    h2 = _rms(x1, gffn_ref[...])
    h2t_ref[...] = _bf(h2.T)
    hh, hl = _split2(h2)
    q = _dot3(hh, hl, wqh_ref[...], wql_ref[...], _dot)
    for p in range(2 * PEER_HEADS):
        qh, ql = _split2(q[:, p * PEER_HALF:(p + 1) * PEER_HALF])
        sp = _dot3(kh_ref[p], kl_ref[p], qh, ql, _dot_nt)
        for c in range(sp.shape[1] // LANES):
            st_ref[p, c] = sp[:, c * LANES:(c + 1) * LANES]


def _out_proj(x, omla, oret, ohg, wout, gffn, wqh, wql, kh, kl, tm):
    n = x.shape[0]
    full = lambda a: pl.BlockSpec(a.shape, lambda i: (0,) * a.ndim)
    row = lambda w: pl.BlockSpec((tm, w), lambda i: (i, 0))
    return pl.pallas_call(
        _out_proj_kernel,
        grid=(n // tm,),
        in_specs=[row(D_MODEL), row(MLA_HEADS * V_HEAD), row(REC_W), row(REC_W), full(wout), full(gffn), full(wqh),
                  full(wql), full(kh), full(kl)],
        out_specs=(row(D_MODEL), pl.BlockSpec((D_MODEL, tm), lambda i: (0, i)),
                   pl.BlockSpec((2 * PEER_HEADS, tm // LANES, PEER_NKEYS, LANES), lambda i: (0, i, 0, 0))),
        out_shape=(jax.ShapeDtypeStruct((n, D_MODEL), jnp.float32),
                   jax.ShapeDtypeStruct((D_MODEL, n), jnp.bfloat16),
                   jax.ShapeDtypeStruct((2 * PEER_HEADS, n // LANES, PEER_NKEYS, LANES), jnp.float32)),
        compiler_params=pltpu.CompilerParams(dimension_semantics=("parallel",), vmem_limit_bytes=VMEM_LIMIT),
        name="out_proj_peer_scores",
    )(x, omla, oret, ohg, wout, gffn, wqh, wql, kh, kl)


def _extract_top(x, count, on_value):
    rowf = lax.broadcasted_iota(jnp.int32, x.shape, 1).astype(jnp.float32)
    for r in range(count):
        m = jnp.max(x, axis=1, keepdims=True)
        first = jnp.min(jnp.where(x == m, rowf, float(x.shape[1])), axis=1, keepdims=True)
        hit = rowf == first
        on_value(r, m, hit)
        x = jnp.where(hit, NEG_INF, x)
    return x


SUB = 8


def _peer_select_kernel(s_ref, theta_ref, coef_ref, e2_ref, v1_ref, v2_ref, cand_ref):
    s1 = s_ref[:, 0]
    s2 = s_ref[:, 1]

    def keep(ref):
        def on_value(r, m, hit):
            ref[:, r:r + 1, :] = m
        return on_value

    _extract_top(s1, PEER_TOPK + 1, keep(v1_ref))
    _extract_top(s2, PEER_TOPK + 1, keep(v2_ref))
    for r1 in range(SUB):
        cand_ref[:, r1 * SUB:(r1 + 1) * SUB, :] = v1_ref[:, r1:r1 + 1, :] + v2_ref[:, 0:SUB, :]
    cand_ref[:, SUB * SUB:SUB * SUB + SUB, :] = v1_ref[:, 0:1, :] + v2_ref[:, SUB:2 * SUB, :]
    cand_ref[:, SUB * SUB + SUB:SUB * SUB + 2 * SUB, :] = v1_ref[:, SUB:2 * SUB, :] + v2_ref[:, 0:1, :]
    best = v1_ref[:, 0:1, :] + v2_ref[:, 0:1, :]
    picked = []
    rest = _extract_top(cand_ref[...], PEER_TOPK, lambda r, m, hit: picked.append(m))
    z = jnp.zeros_like(best)
    for m in picked:
        z = z + jnp.exp(m - best)
    nxt = jnp.max(rest, axis=1, keepdims=True)
    nxt = jnp.maximum(nxt, v1_ref[:, PEER_TOPK:PEER_TOPK + 1, :] + v2_ref[:, 0:1, :])
    nxt = jnp.maximum(nxt, v1_ref[:, 0:1, :] + v2_ref[:, PEER_TOPK:PEER_TOPK + 1, :])
    tau = 0.5 * (picked[-1] + nxt)
    theta_ref[...] = tau - s1
    coef_ref[...] = jnp.exp(s1 - v1_ref[:, 0:1, :]) * (1.0 / z)
    e2_ref[...] = jnp.exp(s2 - v2_ref[:, 0:1, :])


def _peer_select(st):
    nt = st.shape[1]
    s5 = st.reshape(PEER_HEADS, 2, nt, PEER_NKEYS, LANES)
    out = pl.BlockSpec((PEER_HEADS, None, PEER_NKEYS, LANES), lambda t: (0, t, 0, 0))
    shape = jax.ShapeDtypeStruct((PEER_HEADS, nt, PEER_NKEYS, LANES), jnp.float32)
    vrows = SUB * ((PEER_TOPK + 1 + SUB - 1) // SUB)
    return pl.pallas_call(
        _peer_select_kernel,
        grid=(nt,),
        in_specs=[pl.BlockSpec((PEER_HEADS, 2, None, PEER_NKEYS, LANES), lambda t: (0, 0, t, 0, 0))],
        out_specs=(out, out, out),
        out_shape=(shape, shape, shape),
        scratch_shapes=[pltpu.VMEM((PEER_HEADS, vrows, LANES), jnp.float32),
                        pltpu.VMEM((PEER_HEADS, vrows, LANES), jnp.float32),
                        pltpu.VMEM((PEER_HEADS, SUB * SUB + 2 * SUB, LANES), jnp.float32)],
        compiler_params=pltpu.CompilerParams(dimension_semantics=("parallel",), vmem_limit_bytes=VMEM_LIMIT),
        name="peer_select",
    )(s5)


def _peer_expert_kernel(h2t_ref, u_ref, vt_ref, s2_ref, e2_ref, theta_ref, coef_ref, x1_ref, gfin_ref, o_ref,
                        acc_ref, p_ref, at_ref, *, ni, final_norm):
    step = pl.program_id(1)

    @pl.when(step == 0)
    def _():
        acc_ref[...] = jnp.zeros_like(acc_ref)

    at_ref[...] = _dot(u_ref[...], h2t_ref[...])
    for c in range(at_ref.shape[1] // LANES):
        cs = slice(c * LANES, (c + 1) * LANES)
        for ib in range(ni):
            i = step * ni + ib
            rows = slice(ib * PEER_NKEYS, (ib + 1) * PEER_NKEYS)
            w = None
            for h in range(PEER_HEADS):
                th = theta_ref[h, c, pl.ds(i, 1), :]
                cf = coef_ref[h, c, pl.ds(i, 1), :]
                term = jnp.where(s2_ref[h, c] >= th, e2_ref[h, c] * cf, 0.0)
                w = term if w is None else w + term
            a = at_ref[rows, cs]
            gelu = 0.5 * a * (1.0 + lax.erf(a * (2.0 ** -0.5)))
            p_ref[rows, cs] = _bf(gelu * w)
    acc_ref[...] += _dot(vt_ref[...], p_ref[...])

    @pl.when(step == pl.num_programs(1) - 1)
    def _():
        x2 = x1_ref[...] + acc_ref[...].T
        o_ref[...] = _rms(x2, gfin_ref[...]) if final_norm else x2


def _peer_experts(h2t, u, vt, s5, e2, theta, coef, x1, gfin, tm, ni, final_norm):
    n = x1.shape[0]
    et = ni * PEER_NKEYS
    per_head = pl.BlockSpec((PEER_HEADS, tm // LANES, PEER_NKEYS, LANES), lambda t, i: (0, t, 0, 0))
    return pl.pallas_call(
        functools.partial(_peer_expert_kernel, ni=ni, final_norm=final_norm),
        grid=(n // tm, PEER_NKEYS // ni),
        in_specs=[pl.BlockSpec((D_MODEL, tm), lambda t, i: (0, t)),
                  pl.BlockSpec((et, D_MODEL), lambda t, i: (i, 0)),
                  pl.BlockSpec((D_MODEL, et), lambda t, i: (0, i)),
                  pl.BlockSpec((PEER_HEADS, None, tm // LANES, PEER_NKEYS, LANES), lambda t, i: (0, 1, t, 0, 0)),
                  per_head, per_head, per_head,
                  pl.BlockSpec((tm, D_MODEL), lambda t, i: (t, 0)),
                  pl.BlockSpec(gfin.shape, lambda t, i: (0, 0))],
        out_specs=pl.BlockSpec((tm, D_MODEL), lambda t, i: (t, 0)),
        out_shape=jax.ShapeDtypeStruct((n, D_MODEL), jnp.float32),
        scratch_shapes=[pltpu.VMEM((D_MODEL, tm), jnp.float32), pltpu.VMEM((et, tm), jnp.bfloat16),
                        pltpu.VMEM((et, tm), jnp.float32)],
        compiler_params=pltpu.CompilerParams(dimension_semantics=("parallel", "arbitrary"),
                                             vmem_limit_bytes=VMEM_LIMIT),
        name="peer_experts",
    )(h2t, u, vt, s5, e2, theta, coef, x1, gfin)


TOKEN_TILE = 256
ATTN_BLOCK = 512
REC_CHUNK = 128
SAMPLE_CHUNK = 16
SAMPLE_SEQS_PER_STEP = 8
PAGES_PER_STEP = 64
EXPERT_TOKEN_TILE = 512
EXPERT_ROWS_PER_STEP = 8


def kernel(x_prompt, x_sample, cache_kv_latent, cache_k_rope, state_retention, state_hgrn, page_table,
           w_in, q_norm, w_uq, kv_norm, w_uk, w_uv, ret_norm, hg_norm, hg_lower_bounds, w_out,
           attn_norm, ffn_norm, final_norm, peer_wq, peer_keys, peer_u, peer_v):
    batch, seq, d = x_prompt.shape
    dec_batch, td, _ = x_sample.shape
    depth = w_in.shape[0]
    n_p, n_s = batch * seq, dec_batch * td
    past_len = page_table.shape[1] * PAGE_SIZE
    f32 = jnp.float32

    x = jnp.concatenate([x_prompt.reshape(n_p, d), x_sample.reshape(n_s, d)], axis=0)
    pos = jnp.concatenate([jnp.tile(jnp.arange(seq, dtype=jnp.int32), batch),
                           jnp.tile(past_len + jnp.arange(td, dtype=jnp.int32), dec_batch)])
    tabs = _rope_tables(pos)
    cache_krt = jnp.swapaxes(cache_k_rope, 2, 3)
    pages_per_step = math.gcd(PAGES_PER_STEP, page_table.shape[1])
    lb_soft = jax.nn.softmax(hg_lower_bounds.astype(f32), axis=0)
    lbs = jnp.cumsum(lb_soft, axis=0) - lb_soft[0]
    zeros_state = jnp.zeros((batch, RET_HEADS, RET_DK, RET_DV), f32)
    nch = seq // REC_CHUNK
    row2 = lambda a: a.reshape(1, -1)

    outs = {k: [] for k in ("p_ckv", "p_kr", "p_ret", "p_hg", "s_ckv", "s_kr", "s_ret", "s_hg")}
    for l in range(depth):
        wbig, wuq, wuk, wuv = _prep_layer_weights(w_in[l], w_uq[l], w_uk[l], w_uv[l])
        ckv, kr, q, k, v, zp = _proj_in(x, row2(attn_norm[l]), wbig, row2(q_norm[l]), row2(kv_norm[l]), wuq, wuk,
                                        wuv, row2(lbs[l]), tabs, TOKEN_TILE)
        o_p = _attn_prompt(q, k, v, batch, seq, ATTN_BLOCK)
        o_s = _attn_sample(page_table, q, _bf(jnp.transpose(w_uk[l], (1, 0, 2))), _place_wuv(w_uv[l]), ckv, kr,
                           cache_kv_latent, cache_krt, l, n_p, dec_batch, td, pages_per_step)
        o_mla = jnp.concatenate([o_p, _bf(o_s)], axis=0)
        zp_p = zp.reshape(-1, REC_CHUNK, zp.shape[-1])
        zp_s = jnp.pad(zp[n_p:].reshape(dec_batch, td, -1), ((0, 0), (0, SAMPLE_CHUNK - td), (0, 0)))
        gn_ret, gn_hg = row2(ret_norm[l]), row2(hg_norm[l])
        ret_p, ret_sp = _retention(zp_p, zeros_state, gn_ret, 1, nch, REC_CHUNK)
        ret_s, ret_ss = _retention(zp_s, state_retention[l], gn_ret, SAMPLE_SEQS_PER_STEP, 1, td)
        hg_p, hg_sp = _hgrn(zp_p, zeros_state, gn_hg, 1, nch)
        hg_s, hg_ss = _hgrn(zp_s, state_hgrn[l], gn_hg, SAMPLE_SEQS_PER_STEP, 1)
        join = lambda p, s: jnp.concatenate([p.reshape(n_p, REC_W), s[:, :td].reshape(n_s, REC_W)], axis=0)
        o_ret, o_hg = join(ret_p, ret_s), join(hg_p, hg_s)
        wqh, wql = _split2(peer_wq[l])
        kh, kl = _split2(peer_keys[l].reshape(2 * PEER_HEADS, PEER_NKEYS, PEER_HALF))
        x1, h2t, st = _out_proj(x, o_mla, o_ret, o_hg, _bf(w_out[l]), row2(ffn_norm[l]), wqh, wql, kh, kl,
                                TOKEN_TILE)
        theta, coef, e2 = _peer_select(st)
        x = _peer_experts(h2t, _bf(peer_u[l]), _bf(peer_v[l].T), st.reshape((PEER_HEADS, 2) + st.shape[1:]), e2,
                          theta, coef, x1, row2(final_norm), EXPERT_TOKEN_TILE, EXPERT_ROWS_PER_STEP,
                          l == depth - 1)
        outs["p_ckv"].append(ckv[:n_p].reshape(batch, seq, KV_LORA))
        outs["p_kr"].append(kr[:n_p].reshape(batch, seq, QK_ROPE))
        outs["p_ret"].append(ret_sp)
        outs["p_hg"].append(hg_sp)
        outs["s_ckv"].append(ckv[n_p:].reshape(dec_batch, td, KV_LORA))
        outs["s_kr"].append(kr[n_p:].reshape(dec_batch, td, QK_ROPE))
        outs["s_ret"].append(ret_ss)
        outs["s_hg"].append(hg_ss)
    y_prompt = x[:n_p].reshape(batch, seq, d)
    y_sample = x[n_p:].reshape(dec_batch, td, d)
    return (y_prompt, y_sample) + tuple(jnp.stack(outs[k]) for k in
                                        ("p_ckv", "p_kr", "p_ret", "p_hg", "s_ckv", "s_kr", "s_ret", "s_hg"))
```

```python
import functools
import math

import jax
import jax.numpy as jnp
from jax import lax
from jax.experimental import pallas as pl
from jax.experimental.pallas import tpu as pltpu

D_MODEL = 1024
MLA_HEADS = 8
QK_NOPE = 64
QK_ROPE = 32
V_HEAD = 64
Q_LORA = 256
KV_LORA = 128
RET_HEADS = 4
RET_DK = 64
RET_DV = 64
HG_HEADS = 4
HG_DK = 64
HG_DV = 64
PEER_HEADS = 8
PEER_NKEYS = 128
PEER_HALF = 128
PEER_TOPK = 16
ROPE_BASE = 10000.0
EPS = 1e-6
PAGE_SIZE = 128

LANES = 128
HEAD_SLOT = 128
REC_W = 256
NEG_INF = float("-inf")
VMEM_LIMIT = 56 * 1024 * 1024

ZP_RQ, ZP_RK, ZP_RV, ZP_RG, ZP_GQ, ZP_GF, ZP_GK, ZP_GI, ZP_GG = range(9)
ZP_GROUPS = 9


def _bf(x):
    return x.astype(jnp.bfloat16)


def _dot(a, b):
    return jnp.dot(a, b, preferred_element_type=jnp.float32)


def _dot_nt(a, b):
    return lax.dot_general(a, b, (((1,), (1,)), ((), ())), preferred_element_type=jnp.float32)


def _dot_tn(a, b):
    return lax.dot_general(a, b, (((0,), (0,)), ((), ())), preferred_element_type=jnp.float32)


def _rms(x, g):
    return x * lax.rsqrt(jnp.mean(x * x, axis=-1, keepdims=True) + EPS) * g


def _silu(x):
    return x * (1.0 / (1.0 + jnp.exp(-x)))


def _split3(x):
    x1 = _bf(x)
    r1 = x - x1.astype(jnp.float32)
    x2 = _bf(r1)
    x3 = _bf(r1 - x2.astype(jnp.float32))
    return x1, x2, x3


def _proj_in_kernel(x_ref, gattn_ref, wbig_ref, qn_ref, kvn_ref, wuq_ref, wuk_ref, wuv_ref, lb_ref,
                    cq_ref, sq_ref, c64_ref, s64_ref,
                    ckv_ref, kr_ref, q_ref, k_ref, v_ref, zp_ref):
    scale = (QK_NOPE + QK_ROPE) ** -0.5 * math.log2(math.e)
    h = _bf(_rms(x_ref[...], gattn_ref[...]))
    z = _dot(h, wbig_ref[...])
    o = 0
    cq = z[:, o:o + Q_LORA]; o += Q_LORA
    ckv = z[:, o:o + KV_LORA]; o += KV_LORA
    krp = z[:, o:o + HEAD_SLOT]; o += HEAD_SLOT
    krp_rot = z[:, o:o + HEAD_SLOT]; o += HEAD_SLOT
    groups = []
    for _ in range(10):
        groups.append(z[:, o:o + REC_W]); o += REC_W
    rq, rq_rot, rk, rk_rot, rv, rg, gq, gf, gi, gg = groups

    cq_t = cq_ref[...]
    sq_t = sq_ref[...]
    cqn = _bf(_rms(cq, qn_ref[...]))
    zq = _dot(cqn, wuq_ref[...])
    hw = MLA_HEADS * HEAD_SLOT
    for hd in range(MLA_HEADS):
        a = zq[:, hd * HEAD_SLOT:(hd + 1) * HEAD_SLOT]
        b = zq[:, hw + hd * HEAD_SLOT:hw + (hd + 1) * HEAD_SLOT]
        q_ref[hd] = _bf((a * cq_t + b * sq_t) * scale)
    ckvn = _rms(ckv, kvn_ref[...])
    ckv_ref[...] = ckvn
    kr_placed = krp * cq_t + krp_rot * sq_t
    kr_ref[...] = kr_placed[:, QK_NOPE:QK_NOPE + QK_ROPE]
    ckvb = _bf(ckvn)
    kn = _dot(ckvb, wuk_ref[...])
    for hd in range(MLA_HEADS):
        k_ref[hd] = _bf(kn[:, hd * HEAD_SLOT:(hd + 1) * HEAD_SLOT] + kr_placed)
    vv = _dot(ckvb, wuv_ref[...])
    lane = lax.broadcasted_iota(jnp.int32, (vv.shape[0], HEAD_SLOT), 1)
    for hd in range(MLA_HEADS):
        v_ref[hd] = _bf(jnp.where(lane < V_HEAD, vv[:, hd * HEAD_SLOT:(hd + 1) * HEAD_SLOT], 1.0))

    c64 = c64_ref[...]
    s64 = s64_ref[...]
    zp_ref[:, ZP_RQ * REC_W:(ZP_RQ + 1) * REC_W] = rq * c64 + rq_rot * s64
    zp_ref[:, ZP_RK * REC_W:(ZP_RK + 1) * REC_W] = (rk * c64 + rk_rot * s64) * (RET_DK ** -0.5)
    zp_ref[:, ZP_RV * REC_W:(ZP_RV + 1) * REC_W] = rv
    zp_ref[:, ZP_RG * REC_W:(ZP_RG + 1) * REC_W] = rg
    lb = lb_ref[...]
    sig = 1.0 / (1.0 + jnp.exp(-gf))
    zp_ref[:, ZP_GQ * REC_W:(ZP_GQ + 1) * REC_W] = _silu(gq)
    zp_ref[:, ZP_GF * REC_W:(ZP_GF + 1) * REC_W] = jnp.log(lb + (1.0 - lb) * sig)
    zp_ref[:, ZP_GK * REC_W:(ZP_GK + 1) * REC_W] = (1.0 - lb) * (1.0 / (1.0 + jnp.exp(gf)))
    zp_ref[:, ZP_GI * REC_W:(ZP_GI + 1) * REC_W] = gi
    zp_ref[:, ZP_GG * REC_W:(ZP_GG + 1) * REC_W] = gg


def _rot_cols(w, n_heads, d):
    k = w.shape[0]
    w4 = w.reshape(k, n_heads, 2, d // 2)
    return jnp.concatenate([-w4[:, :, 1:2], w4[:, :, 0:1]], axis=2).reshape(k, n_heads * d)


def _prep_layer_weights(w_in, w_uq, w_uk, w_uv):
    sizes = (Q_LORA, KV_LORA, QK_ROPE) + (REC_W,) * 8
    offs = [0]
    for s in sizes:
        offs.append(offs[-1] + s)
    cols = [w_in[:, offs[i]:offs[i + 1]] for i in range(len(sizes))]
    cq, ckv, kr, rq, rk, rv, rg, gq, gf, gi, gg = cols
    kdim = w_in.shape[0]
    zeros = lambda n: jnp.zeros((kdim, n), w_in.dtype)
    place = lambda c: jnp.concatenate([zeros(QK_NOPE), c, zeros(HEAD_SLOT - QK_NOPE - QK_ROPE)], axis=1)
    wbig = jnp.concatenate([
        cq, ckv, place(kr), place(_rot_cols(kr, 1, QK_ROPE)),
        rq, _rot_cols(rq, RET_HEADS, RET_DK), rk, _rot_cols(rk, RET_HEADS, RET_DK),
        rv, rg, gq, gf, gi, gg], axis=1)
    wq3 = w_uq.reshape(Q_LORA, MLA_HEADS, QK_NOPE + QK_ROPE)
    nope, ropep = wq3[..., :QK_NOPE], wq3[..., QK_NOPE:]
    rope_rot = jnp.concatenate([-ropep[..., QK_ROPE // 2:], ropep[..., :QK_ROPE // 2]], axis=-1)
    pad = jnp.zeros((Q_LORA, MLA_HEADS, HEAD_SLOT - QK_NOPE - QK_ROPE), w_uq.dtype)
    plain = jnp.concatenate([nope, ropep, pad], axis=-1).reshape(Q_LORA, MLA_HEADS * HEAD_SLOT)
    rotd = jnp.concatenate([jnp.zeros_like(nope), rope_rot, pad], axis=-1).reshape(Q_LORA, MLA_HEADS * HEAD_SLOT)
    wuq = jnp.concatenate([plain, rotd], axis=1)
    wuk = jnp.concatenate([w_uk, jnp.zeros((KV_LORA, MLA_HEADS, HEAD_SLOT - QK_NOPE), w_uk.dtype)],
                          axis=-1).reshape(KV_LORA, MLA_HEADS * HEAD_SLOT)
    wuv = jnp.concatenate([w_uv, jnp.zeros((KV_LORA, MLA_HEADS, HEAD_SLOT - V_HEAD), w_uv.dtype)],
                          axis=-1).reshape(KV_LORA, MLA_HEADS * HEAD_SLOT)
    return _bf(wbig), _bf(wuq), _bf(wuk), _bf(wuv)


def _rope_tables(pos):
    posf = pos.astype(jnp.float32)[:, None]

    def cs(d):
        half = d // 2
        inv = ROPE_BASE ** (-jnp.arange(half, dtype=jnp.float32) / half)
        ang = posf * inv[None, :]
        return jnp.cos(ang), jnp.sin(ang)

    n = pos.shape[0]
    c32, s32 = cs(QK_ROPE)
    tail = jnp.zeros((n, HEAD_SLOT - QK_NOPE - QK_ROPE), jnp.float32)
    cq = jnp.concatenate([jnp.ones((n, QK_NOPE), jnp.float32), c32, c32, tail], axis=1)
    sq = jnp.concatenate([jnp.zeros((n, QK_NOPE), jnp.float32), s32, s32, tail], axis=1)
    c64, s64 = cs(RET_DK)
    c64 = jnp.tile(jnp.concatenate([c64, c64], axis=1), (1, RET_HEADS))
    s64 = jnp.tile(jnp.concatenate([s64, s64], axis=1), (1, RET_HEADS))
    return cq, sq, c64, s64


def _proj_in(x, gattn, wbig, qn, kvn, wuq, wuk, wuv, lb, tabs, tm):
    n = x.shape[0]
    cq_t, sq_t, c64, s64 = tabs
    full = lambda a: pl.BlockSpec(a.shape, lambda i: (0,) * a.ndim)
    row = lambda w: pl.BlockSpec((tm, w), lambda i: (i, 0))
    hrow = lambda nh: pl.BlockSpec((nh, tm, HEAD_SLOT), lambda i: (0, i, 0))
    out_shape = (
        jax.ShapeDtypeStruct((n, KV_LORA), jnp.float32),
        jax.ShapeDtypeStruct((n, QK_ROPE), jnp.float32),
        jax.ShapeDtypeStruct((MLA_HEADS, n, HEAD_SLOT), jnp.bfloat16),
        jax.ShapeDtypeStruct((MLA_HEADS, n, HEAD_SLOT), jnp.bfloat16),
        jax.ShapeDtypeStruct((MLA_HEADS, n, HEAD_SLOT), jnp.bfloat16),
        jax.ShapeDtypeStruct((n, ZP_GROUPS * REC_W), jnp.float32),
    )
    return pl.pallas_call(
        _proj_in_kernel,
        grid=(n // tm,),
        in_specs=[row(D_MODEL), full(gattn), full(wbig), full(qn), full(kvn), full(wuq), full(wuk), full(wuv),
                  full(lb), row(HEAD_SLOT), row(HEAD_SLOT), row(REC_W), row(REC_W)],
        out_specs=(row(KV_LORA), row(QK_ROPE), hrow(MLA_HEADS), hrow(MLA_HEADS), hrow(MLA_HEADS),
                   row(ZP_GROUPS * REC_W)),
        out_shape=out_shape,
        compiler_params=pltpu.CompilerParams(dimension_semantics=("parallel",), vmem_limit_bytes=VMEM_LIMIT),
        name="proj_in",
    )(x, gattn, wbig, qn, kvn, wuq, wuk, wuv, lb, cq_t, sq_t, c64, s64)


def _online_step(s, vb, m, acc):
    m_new = jnp.maximum(m, jnp.max(s, axis=-1, keepdims=True))
    acc = jnp.exp2(m - m_new) * acc + _dot(_bf(jnp.exp2(s - m_new)), vb)
    return m_new, acc


def _attn_prompt_kernel(q_ref, k_ref, v_ref, o_ref, *, blk):
    qi = pl.program_id(2)
    qs = (q_ref[0], q_ref[1])

    def step(j, carry, masked):
        start = pl.multiple_of(j * blk, blk)
        out = []
        for hh in range(2):
            s = _dot_nt(qs[hh], k_ref[hh, pl.ds(start, blk), :])
            if masked:
                row = lax.broadcasted_iota(jnp.int32, (blk, blk), 0)
                col = lax.broadcasted_iota(jnp.int32, (blk, blk), 1)
                s = jnp.where(col <= row, s, NEG_INF)
            out.append(_online_step(s, v_ref[hh, pl.ds(start, blk), :], *carry[hh]))
        return tuple(out)

    init = (jnp.full((blk, 1), NEG_INF, jnp.float32), jnp.zeros((blk, LANES), jnp.float32))
    carry = lax.fori_loop(0, qi, lambda j, c: step(j, c, False), (init, init))
    carry = step(qi, carry, True)
    norm = [acc / pltpu.roll(acc, V_HEAD, axis=1) for _, acc in carry]
    lane = lax.broadcasted_iota(jnp.int32, (blk, LANES), 1)
    o_ref[...] = _bf(jnp.where(lane < V_HEAD, norm[0], pltpu.roll(norm[1], V_HEAD, axis=1)))


def _attn_prompt(q, k, v, batch, seq, blk):
    nq = seq // blk
    return pl.pallas_call(
        functools.partial(_attn_prompt_kernel, blk=blk),
        grid=(batch, MLA_HEADS // 2, nq),
        in_specs=[pl.BlockSpec((2, blk, HEAD_SLOT), lambda b, p, i: (p, b * nq + i, 0)),
                  pl.BlockSpec((2, seq, HEAD_SLOT), lambda b, p, i: (p, b, 0)),
                  pl.BlockSpec((2, seq, LANES), lambda b, p, i: (p, b, 0))],
        out_specs=pl.BlockSpec((blk, LANES), lambda b, p, i: (b * nq + i, p)),
        out_shape=jax.ShapeDtypeStruct((batch * seq, MLA_HEADS * V_HEAD), jnp.bfloat16),
        compiler_params=pltpu.CompilerParams(dimension_semantics=("parallel", "parallel", "arbitrary"),
                                             vmem_limit_bytes=VMEM_LIMIT),
        name="attn_prompt",
    )(q, k, v)


def _attn_sample_kernel(pt_ref, q_ref, wuk_ref, wuvp_ref, ckv_ref, kr_ref, *rest, pages_per_step, td):
    kv_refs = rest[:pages_per_step]
    krt_refs = rest[pages_per_step:2 * pages_per_step]
    o_ref = rest[2 * pages_per_step]
    qlat_ref, qrope_ref, m_ref, l_ref, acc_ref, kall_ref, krt_ref = rest[2 * pages_per_step + 1:]
    j = pl.program_id(1)
    rows = MLA_HEADS * td

    @pl.when(j == 0)
    def _():
        for hd in range(MLA_HEADS):
            qh = q_ref[hd].astype(jnp.float32)
            qlat_ref[hd * td:(hd + 1) * td, :] = _dot_nt(_bf(qh[:, :QK_NOPE]), wuk_ref[hd])
            qrope_ref[hd * td:(hd + 1) * td, :] = qh[:, QK_NOPE:QK_NOPE + QK_ROPE]
        m_ref[...] = jnp.full((rows, 1), NEG_INF, jnp.float32)
        l_ref[...] = jnp.zeros((rows, 1), jnp.float32)
        acc_ref[...] = jnp.zeros((rows, KV_LORA), jnp.float32)

    for r in range(pages_per_step):
        kall_ref[r * PAGE_SIZE:(r + 1) * PAGE_SIZE, :] = _bf(kv_refs[r][...])
        krt_ref[:, r * PAGE_SIZE:(r + 1) * PAGE_SIZE] = _bf(krt_refs[r][...])
    qlat = _bf(qlat_ref[...])
    qrope = _bf(qrope_ref[...])
    kall = kall_ref[...]
    s = _dot_nt(qlat, kall) + _dot(qrope, krt_ref[...])
    m = m_ref[...]
    m_new = jnp.maximum(m, jnp.max(s, axis=-1, keepdims=True))
    alpha = jnp.exp2(m - m_new)
    p = jnp.exp2(s - m_new)
    l = alpha * l_ref[...] + jnp.sum(p, axis=-1, keepdims=True)
    acc = alpha * acc_ref[...] + _dot(_bf(p), kall)
    m_ref[...], l_ref[...], acc_ref[...] = m_new, l, acc

    @pl.when(j == pl.num_programs(1) - 1)
    def _():
        pad = jnp.zeros((PAGE_SIZE - td, KV_LORA), jnp.float32)
        cnew = _bf(jnp.concatenate([ckv_ref[...], pad], axis=0))
        krnew = _bf(jnp.concatenate([kr_ref[...], pad[:, :QK_ROPE]], axis=0))
        sn = _dot_nt(qlat, cnew) + _dot_nt(qrope, krnew)
        row = lax.broadcasted_iota(jnp.int32, (rows, PAGE_SIZE), 0)
        col = lax.broadcasted_iota(jnp.int32, (rows, PAGE_SIZE), 1)
        sn = jnp.where(col <= row % td, sn, NEG_INF)
        m2 = jnp.maximum(m_new, jnp.max(sn, axis=-1, keepdims=True))
        a2 = jnp.exp2(m_new - m2)
        pn = jnp.exp2(sn - m2)
        l2 = a2 * l + jnp.sum(pn, axis=-1, keepdims=True)
        acc2 = a2 * acc + _dot(_bf(pn), cnew)
        olat = _bf(acc2 / l2)
        out = jnp.zeros((td, MLA_HEADS * V_HEAD), jnp.float32)
        for hd in range(MLA_HEADS):
            out = out + _dot(olat[hd * td:(hd + 1) * td, :], wuvp_ref[hd])
        o_ref[...] = out


def _attn_sample(page_table, q, wuk_h, wuv_placed, ckv, kr, cache_kv, cache_krt, layer, row0, dec_batch, td,
                 pages_per_step):
    n_pages = page_table.shape[1]
    steps = n_pages // pages_per_step
    blk0 = row0 // td
    rows = MLA_HEADS * td
    keys = pages_per_step * PAGE_SIZE

    def page_spec(shape, r):
        return pl.BlockSpec((None, None) + shape,
                            lambda b, j, pt, r=r: (layer, pt[b * n_pages + j * pages_per_step + r], 0, 0))

    in_specs = [pl.BlockSpec((MLA_HEADS, td, HEAD_SLOT), lambda b, j, pt: (0, blk0 + b, 0)),
                pl.BlockSpec(wuk_h.shape, lambda b, j, pt: (0, 0, 0)),
                pl.BlockSpec(wuv_placed.shape, lambda b, j, pt: (0, 0, 0)),
                pl.BlockSpec((td, KV_LORA), lambda b, j, pt: (blk0 + b, 0)),
                pl.BlockSpec((td, QK_ROPE), lambda b, j, pt: (blk0 + b, 0))]
    in_specs += [page_spec((PAGE_SIZE, KV_LORA), r) for r in range(pages_per_step)]
    in_specs += [page_spec((QK_ROPE, PAGE_SIZE), r) for r in range(pages_per_step)]
    grid_spec = pltpu.PrefetchScalarGridSpec(
        num_scalar_prefetch=1,
        grid=(dec_batch, steps),
        in_specs=in_specs,
        out_specs=pl.BlockSpec((td, MLA_HEADS * V_HEAD), lambda b, j, pt: (b, 0)),
        scratch_shapes=[pltpu.VMEM((rows, KV_LORA), jnp.float32), pltpu.VMEM((rows, QK_ROPE), jnp.float32),
                        pltpu.VMEM((rows, 1), jnp.float32), pltpu.VMEM((rows, 1), jnp.float32),
                        pltpu.VMEM((rows, KV_LORA), jnp.float32),
                        pltpu.VMEM((keys, KV_LORA), jnp.bfloat16), pltpu.VMEM((QK_ROPE, keys), jnp.bfloat16)])
    return pl.pallas_call(
        functools.partial(_attn_sample_kernel, pages_per_step=pages_per_step, td=td),
        grid_spec=grid_spec,
        out_shape=jax.ShapeDtypeStruct((dec_batch * td, MLA_HEADS * V_HEAD), jnp.float32),
        compiler_params=pltpu.CompilerParams(dimension_semantics=("parallel", "arbitrary"),
                                             vmem_limit_bytes=VMEM_LIMIT),
        name="attn_sample",
    )(page_table.reshape(-1), q, wuk_h, wuv_placed, ckv, kr, *([cache_kv] * pages_per_step),
      *([cache_krt] * pages_per_step))


def _place_wuv(w_uv):
    eye = jnp.eye(MLA_HEADS, dtype=w_uv.dtype)
    placed = jnp.einsum('chv,hg->hcgv', w_uv, eye)
    return _bf(placed.reshape(MLA_HEADS, KV_LORA, MLA_HEADS * V_HEAD))


def _ret_gammas():
    return [1.0 - 2.0 ** (-5.0 - h) for h in range(RET_HEADS)]


def _ret_kernel(*refs, nb, parts, decay_len):
    q_refs, k_refs, v_refs, g_refs = (refs[g * parts:(g + 1) * parts] for g in range(4))
    s0_ref, dmask_ref, qdec_ref, kdec_ref, gn_ref, o_ref, sfin_ref, s_ref = refs[4 * parts:]
    c = pl.program_id(1)

    @pl.when(c == 0)
    def _():
        s_ref[...] = s0_ref[...]

    qdec = qdec_ref[...]
    kdec = kdec_ref[...]
    gn = gn_ref[...]
    for sq in range(parts * nb):
        p, bi = divmod(sq, nb)
        q = q_refs[p][bi]
        k = k_refs[p][bi]
        v = v_refs[p][bi]
        kd = k * kdec
        gate = _silu(g_refs[p][bi])
        for h in range(RET_HEADS):
            sl = slice(h * RET_DK, (h + 1) * RET_DK)
            qh, kh, vh = _bf(q[:, sl]), _bf(k[:, sl]), _bf(v[:, sl])
            a = _dot_nt(qh, kh) * dmask_ref[h]
            s = s_ref[sq, h]
            o = _dot(_bf(a), vh) + _dot(qh, _bf(s)) * qdec[:, sl]
            s_ref[sq, h] = s * (_ret_gammas()[h] ** decay_len) + _dot_tn(_bf(kd[:, sl]), vh)
            oc = o - jnp.mean(o, axis=-1, keepdims=True)
            y = oc * lax.rsqrt(jnp.mean(oc * oc, axis=-1, keepdims=True) + EPS)
            o_ref[sq, :, sl] = _bf(y * gn[:, sl] * gate[:, sl])

    @pl.when(c == pl.num_programs(1) - 1)
    def _():
        sfin_ref[...] = s_ref[...]


def _ret_consts(chunk, decay_len):
    lg = jnp.log(jnp.asarray(_ret_gammas(), jnp.float32))
    idx = jnp.arange(chunk, dtype=jnp.float32)
    diff = idx[:, None] - idx[None, :]
    dmask = jnp.where(diff >= 0, jnp.exp(lg[:, None, None] * jnp.maximum(diff, 0.0)), 0.0)
    qdec = jnp.exp(lg[None, :] * (idx[:, None] + 1.0))
    kdec = jnp.exp(lg[None, :] * (decay_len - 1.0 - idx[:, None]))
    rep = lambda a: jnp.repeat(a, RET_DK, axis=1)
    return dmask, rep(qdec), rep(kdec)


def _rec_specs(nb, parts, chunk, nch, groups):
    return [pl.BlockSpec((nb, chunk, REC_W), lambda b, c, g=g, p=p: ((b * parts + p) * nch + c, 0, g))
            for g in groups for p in range(parts)]


def _rec_call(kernel_fn, name, groups, zp3, s0, extra, nb, parts, nch):
    chunk = zp3.shape[1]
    seqs = s0.shape[0]
    per_step = parts * nb
    full = lambda a: pl.BlockSpec(a.shape, lambda b, c: (0,) * a.ndim)
    st = pl.BlockSpec((per_step,) + s0.shape[1:], lambda b, c: (b, 0, 0, 0))
    return pl.pallas_call(
        kernel_fn,
        grid=(seqs // per_step, nch),
        in_specs=_rec_specs(nb, parts, chunk, nch, groups) + [st] + [full(a) for a in extra],
        out_specs=(pl.BlockSpec((per_step, None, chunk, REC_W), lambda b, c: (b, c, 0, 0)), st),
        out_shape=(jax.ShapeDtypeStruct((seqs, nch, chunk, REC_W), jnp.bfloat16),
                   jax.ShapeDtypeStruct(s0.shape, jnp.float32)),
        scratch_shapes=[pltpu.VMEM((per_step,) + s0.shape[1:], jnp.float32)],
        compiler_params=pltpu.CompilerParams(dimension_semantics=("parallel", "arbitrary"),
                                             vmem_limit_bytes=VMEM_LIMIT),
        name=name,
    )(*([zp3] * (len(groups) * parts)), s0, *extra)


def _retention(zp3, s0, gn, nb, parts, nch, decay_len):
    consts = _ret_consts(zp3.shape[1], decay_len)
    return _rec_call(functools.partial(_ret_kernel, nb=nb, parts=parts, decay_len=decay_len), "retention",
                     (ZP_RQ, ZP_RK, ZP_RV, ZP_RG), zp3, s0, consts + (gn,), nb, parts, nch)


def _hgrn_sum_matrix(chunk):
    import numpy as np
    t = np.arange(chunk)[:, None]
    r = np.arange(chunk)[None, :]
    blocks = [r <= t, r > t]
    m = 2
    while m <= chunk:
        half = m // 2
        mid = (t // m) * m + half
        upper = (t % m) >= half
        blocks.append(upper & (r >= mid) & (r <= t))
        blocks.append(~upper & (r > t) & (r <= mid - 1))
        m *= 2
    return jnp.asarray(np.concatenate(blocks, axis=0).astype(np.float32), jnp.bfloat16)


def _hgrn_kernel(*refs, nb, parts, chunk):
    q_refs, f_refs, k_refs, v_refs, g_refs = (refs[g * parts:(g + 1) * parts] for g in range(5))
    s0_ref, msum_ref, gn_ref, o_ref, sfin_ref, st_ref = refs[5 * parts:]
    c = pl.program_id(1)
    nlev = chunk.bit_length() - 1

    @pl.when(c == 0)
    def _():
        for bi in range(parts * nb):
            for h in range(HG_HEADS):
                st_ref[bi, h] = s0_ref[bi, h].T

    gn = gn_ref[...]
    rowi = lax.broadcasted_iota(jnp.int32, (chunk, REC_W), 0)
    r2 = lax.broadcasted_iota(jnp.int32, (chunk, chunk), 0)
    c2 = lax.broadcasted_iota(jnp.int32, (chunk, chunk), 1)
    for sq in range(parts * nb):
        p, bi = divmod(sq, nb)
        q = q_refs[p][bi]
        k = k_refs[p][bi]
        v = v_refs[p][bi]
        gate = _silu(g_refs[p][bi])
        f1, f2, f3 = _split3(f_refs[p][bi])
        sums = _dot(msum_ref[...], jnp.concatenate([f1, f2, f3], axis=1))
        sums = sums[:, :REC_W] + sums[:, REC_W:2 * REC_W] + sums[:, 2 * REC_W:]
        part = lambda i: sums[i * chunk:(i + 1) * chunk]
        b = part(0)
        qb = _bf(q * jnp.exp(b))
        ks = _bf(k * jnp.exp(part(1)))
        e_last = jnp.exp(b[chunk - 1:chunk, :])
        vb = _bf(v)
        qs, kls = [_bf(q)], [_bf(k)]
        for lev in range(1, nlev + 1):
            upper = (rowi & (1 << (lev - 1))) != 0
            qs.append(_bf(jnp.where(upper, q * jnp.exp(part(2 * lev)), 0.0)))
            kls.append(_bf(jnp.where(upper, 0.0, k * jnp.exp(part(2 * lev + 1)))))
        for h in range(HG_HEADS):
            sl = slice(h * HG_DK, (h + 1) * HG_DK)
            a = jnp.where(r2 == c2, _dot_nt(qs[0][:, sl], kls[0][:, sl]), 0.0)
            for lev in range(1, nlev + 1):
                same = (r2 >> lev) == (c2 >> lev)
                a = a + jnp.where(same, _dot_nt(qs[lev][:, sl], kls[lev][:, sl]), 0.0)
            st = st_ref[sq, h]
            o = _dot(_bf(a), vb[:, sl]) + _dot_nt(qb[:, sl], _bf(st))
            st_ref[sq, h] = st * e_last[:, sl] + _dot_tn(vb[:, sl], ks[:, sl])
            y = o * lax.rsqrt(jnp.mean(o * o, axis=-1, keepdims=True) + EPS)
            o_ref[sq, :, sl] = _bf(y * gn[:, sl] * gate[:, sl])

    @pl.when(c == pl.num_programs(1) - 1)
    def _():
        for bi in range(parts * nb):
            for h in range(HG_HEADS):
                sfin_ref[bi, h] = st_ref[bi, h].T


def _hgrn(zp3, s0, gn, nb, parts, nch):
    chunk = zp3.shape[1]
    return _rec_call(functools.partial(_hgrn_kernel, nb=nb, parts=parts, chunk=chunk), "hgrn2",
                     (ZP_GQ, ZP_GF, ZP_GK, ZP_GI, ZP_GG), zp3, s0, (_hgrn_sum_matrix(chunk), gn), nb, parts, nch)


def _dot3(ah, al, bh, bl, dot):
    return dot(ah, bh) + dot(al, bh) + dot(ah, bl)


def _split2(x):
    hi = _bf(x)
    return hi, _bf(x - hi.astype(jnp.float32))


def _out_proj_kernel(x_ref, omla_ref, oret_ref, ohg_ref, wout_ref, gffn_ref, wqh_ref, wql_ref, kh_ref, kl_ref,
                     x1_ref, h2t_ref, st_ref):
    mix = jnp.concatenate([omla_ref[...], oret_ref[...], ohg_ref[...]], axis=1)
    x1 = x_ref[...] + _dot(mix, wout_ref[...])
    x1_ref[...] = x1
    h2 = _rms(x1, gffn_ref[...])
    h2t_ref[...] = _bf(h2.T)
    hh, hl = _split2(h2)
    q = _dot3(hh, hl, wqh_ref[...], wql_ref[...], _dot)
    for p in range(2 * PEER_HEADS):
        qh, ql = _split2(q[:, p * PEER_HALF:(p + 1) * PEER_HALF])
        sp = _dot3(kh_ref[p], kl_ref[p], qh, ql, _dot_nt)
        for c in range(sp.shape[1] // LANES):
            st_ref[p, c] = sp[:, c * LANES:(c + 1) * LANES]


def _out_proj(x, omla, oret, ohg, wout, gffn, wqh, wql, kh, kl, tm):
    n = x.shape[0]
    full = lambda a: pl.BlockSpec(a.shape, lambda i: (0,) * a.ndim)
    row = lambda w: pl.BlockSpec((tm, w), lambda i: (i, 0))
    return pl.pallas_call(
        _out_proj_kernel,
        grid=(n // tm,),
        in_specs=[row(D_MODEL), row(MLA_HEADS * V_HEAD), row(REC_W), row(REC_W), full(wout), full(gffn), full(wqh),
                  full(wql), full(kh), full(kl)],
        out_specs=(row(D_MODEL), pl.BlockSpec((D_MODEL, tm), lambda i: (0, i)),
                   pl.BlockSpec((2 * PEER_HEADS, tm // LANES, PEER_NKEYS, LANES), lambda i: (0, i, 0, 0))),
        out_shape=(jax.ShapeDtypeStruct((n, D_MODEL), jnp.float32),
                   jax.ShapeDtypeStruct((D_MODEL, n), jnp.bfloat16),
                   jax.ShapeDtypeStruct((2 * PEER_HEADS, n // LANES, PEER_NKEYS, LANES), jnp.float32)),
        compiler_params=pltpu.CompilerParams(dimension_semantics=("parallel",), vmem_limit_bytes=VMEM_LIMIT),
        name="out_proj_peer_scores",
    )(x, omla, oret, ohg, wout, gffn, wqh, wql, kh, kl)


def _extract_top(x, count, on_value):
    rowf = lax.broadcasted_iota(jnp.int32, x.shape, 1).astype(jnp.float32)
    for r in range(count):
        m = jnp.max(x, axis=1, keepdims=True)
        first = jnp.min(jnp.where(x == m, rowf, float(x.shape[1])), axis=1, keepdims=True)
        hit = rowf == first
        on_value(r, m, hit)
        x = jnp.where(hit, NEG_INF, x)
    return x


SUB = 8


def _peer_select_kernel(s_ref, theta_ref, coef_ref, e2_ref, v1_ref, v2_ref, cand_ref):
    s1 = s_ref[:, 0]
    s2 = s_ref[:, 1]

    def keep(ref):
        def on_value(r, m, hit):
            ref[:, r:r + 1, :] = m
        return on_value

    _extract_top(s1, PEER_TOPK + 1, keep(v1_ref))
    _extract_top(s2, PEER_TOPK + 1, keep(v2_ref))
    for r1 in range(SUB):
        cand_ref[:, r1 * SUB:(r1 + 1) * SUB, :] = v1_ref[:, r1:r1 + 1, :] + v2_ref[:, 0:SUB, :]
    cand_ref[:, SUB * SUB:SUB * SUB + SUB, :] = v1_ref[:, 0:1, :] + v2_ref[:, SUB:2 * SUB, :]
    cand_ref[:, SUB * SUB + SUB:SUB * SUB + 2 * SUB, :] = v1_ref[:, SUB:2 * SUB, :] + v2_ref[:, 0:1, :]
    best = v1_ref[:, 0:1, :] + v2_ref[:, 0:1, :]
    picked = []
    rest = _extract_top(cand_ref[...], PEER_TOPK, lambda r, m, hit: picked.append(m))
    z = jnp.zeros_like(best)
    for m in picked:
        z = z + jnp.exp(m - best)
    nxt = jnp.max(rest, axis=1, keepdims=True)
    nxt = jnp.maximum(nxt, v1_ref[:, PEER_TOPK:PEER_TOPK + 1, :] + v2_ref[:, 0:1, :])
    nxt = jnp.maximum(nxt, v1_ref[:, 0:1, :] + v2_ref[:, PEER_TOPK:PEER_TOPK + 1, :])
    tau = 0.5 * (picked[-1] + nxt)
    theta_ref[...] = tau - s1
    coef_ref[...] = jnp.exp(s1 - v1_ref[:, 0:1, :]) * (1.0 / z)
    e2_ref[...] = jnp.exp(s2 - v2_ref[:, 0:1, :])


def _peer_select(st):
    nt = st.shape[1]
    s5 = st.reshape(PEER_HEADS, 2, nt, PEER_NKEYS, LANES)
    out = pl.BlockSpec((PEER_HEADS, None, PEER_NKEYS, LANES), lambda t: (0, t, 0, 0))
    shape = jax.ShapeDtypeStruct((PEER_HEADS, nt, PEER_NKEYS, LANES), jnp.float32)
    vrows = SUB * ((PEER_TOPK + 1 + SUB - 1) // SUB)
    return pl.pallas_call(
        _peer_select_kernel,
        grid=(nt,),
        in_specs=[pl.BlockSpec((PEER_HEADS, 2, None, PEER_NKEYS, LANES), lambda t: (0, 0, t, 0, 0))],
        out_specs=(out, out, out),
        out_shape=(shape, shape, shape),
        scratch_shapes=[pltpu.VMEM((PEER_HEADS, vrows, LANES), jnp.float32),
                        pltpu.VMEM((PEER_HEADS, vrows, LANES), jnp.float32),
                        pltpu.VMEM((PEER_HEADS, SUB * SUB + 2 * SUB, LANES), jnp.float32)],
        compiler_params=pltpu.CompilerParams(dimension_semantics=("parallel",), vmem_limit_bytes=VMEM_LIMIT),
        name="peer_select",
    )(s5)


def _peer_expert_kernel(h2t_ref, u_ref, vt_ref, s2_ref, e2_ref, theta_ref, coef_ref, x1_ref, gfin_ref, o_ref,
                        acc_ref, p_ref, *, ni, final_norm):
    step = pl.program_id(1)

    @pl.when(step == 0)
    def _():
        acc_ref[...] = jnp.zeros_like(acc_ref)

    h2t = h2t_ref[...]
    tm = h2t.shape[1]
    for g in range(ni // 2):
        at = _dot(u_ref[g * 2 * PEER_NKEYS:(g + 1) * 2 * PEER_NKEYS, :], h2t)
        for half in range(2):
            ib = 2 * g + half
            i = step * ni + ib
            for c in range(tm // LANES):
                cs = slice(c * LANES, (c + 1) * LANES)
                w = None
                for h in range(PEER_HEADS):
                    th = theta_ref[h, c, pl.ds(i, 1), :]
                    cf = coef_ref[h, c, pl.ds(i, 1), :]
                    term = jnp.where(s2_ref[h, c] >= th, e2_ref[h, c] * cf, 0.0)
                    w = term if w is None else w + term
                a = at[half * PEER_NKEYS:(half + 1) * PEER_NKEYS, cs]
                gelu = 0.5 * a * (1.0 + lax.erf(a * (2.0 ** -0.5)))
                p_ref[ib * PEER_NKEYS:(ib + 1) * PEER_NKEYS, cs] = _bf(gelu * w)
    acc_ref[...] += _dot(vt_ref[...], p_ref[...])

    @pl.when(step == pl.num_programs(1) - 1)
    def _():
        x2 = x1_ref[...] + acc_ref[...].T
        o_ref[...] = _rms(x2, gfin_ref[...]) if final_norm else x2


def _peer_experts(h2t, u, vt, s5, e2, theta, coef, x1, gfin, tm, ni, final_norm):
    n = x1.shape[0]
    et = ni * PEER_NKEYS
    blocks = PEER_NKEYS // ni
    per_head = pl.BlockSpec((PEER_HEADS, tm // LANES, PEER_NKEYS, LANES), lambda t, i: (0, t, 0, 0))
    return pl.pallas_call(
        functools.partial(_peer_expert_kernel, ni=ni, final_norm=final_norm),
        grid=(n // tm, blocks),
        in_specs=[pl.BlockSpec((D_MODEL, tm), lambda t, i: (0, t)),
                  pl.BlockSpec((et, D_MODEL), lambda t, i: (i, 0)),
                  pl.BlockSpec((D_MODEL, et), lambda t, i: (0, i)),
                  pl.BlockSpec((PEER_HEADS, None, tm // LANES, PEER_NKEYS, LANES), lambda t, i: (0, 1, t, 0, 0)),
                  per_head, per_head, per_head,
                  pl.BlockSpec((tm, D_MODEL), lambda t, i: (t, 0)),
                  pl.BlockSpec(gfin.shape, lambda t, i: (0, 0))],
        out_specs=pl.BlockSpec((tm, D_MODEL), lambda t, i: (t, 0)),
        out_shape=jax.ShapeDtypeStruct((n, D_MODEL), jnp.float32),
        scratch_shapes=[pltpu.VMEM((D_MODEL, tm), jnp.float32), pltpu.VMEM((et, tm), jnp.bfloat16)],
        compiler_params=pltpu.CompilerParams(dimension_semantics=("parallel", "arbitrary"),
                                             vmem_limit_bytes=VMEM_LIMIT),
        name="peer_experts",
    )(h2t, u, vt, s5, e2, theta, coef, x1, gfin)


TOKEN_TILE = 256
ATTN_BLOCK = 512
REC_CHUNK = 128
SAMPLE_CHUNK = 16
SAMPLE_SEQS_PER_STEP = 8
PAGES_PER_STEP = 64
EXPERT_TOKEN_TILE = 512
EXPERT_ROWS_PER_STEP = 8


def kernel(x_prompt, x_sample, cache_kv_latent, cache_k_rope, state_retention, state_hgrn, page_table,
           w_in, q_norm, w_uq, kv_norm, w_uk, w_uv, ret_norm, hg_norm, hg_lower_bounds, w_out,
           attn_norm, ffn_norm, final_norm, peer_wq, peer_keys, peer_u, peer_v):
    batch, seq, d = x_prompt.shape
    dec_batch, td, _ = x_sample.shape
    depth = w_in.shape[0]
    n_p, n_s = batch * seq, dec_batch * td
    past_len = page_table.shape[1] * PAGE_SIZE
    f32 = jnp.float32

    x = jnp.concatenate([x_prompt.reshape(n_p, d), x_sample.reshape(n_s, d)], axis=0)
    pos = jnp.concatenate([jnp.tile(jnp.arange(seq, dtype=jnp.int32), batch),
                           jnp.tile(past_len + jnp.arange(td, dtype=jnp.int32), dec_batch)])
    tabs = _rope_tables(pos)
    cache_krt = jnp.swapaxes(cache_k_rope, 2, 3)
    pages_per_step = math.gcd(PAGES_PER_STEP, page_table.shape[1])
    lb_soft = jax.nn.softmax(hg_lower_bounds.astype(f32), axis=0)
    lbs = jnp.cumsum(lb_soft, axis=0) - lb_soft[0]
    zeros_state = jnp.zeros((batch, RET_HEADS, RET_DK, RET_DV), f32)
    nch = seq // REC_CHUNK
    row2 = lambda a: a.reshape(1, -1)

    outs = {k: [] for k in ("p_ckv", "p_kr", "p_ret", "p_hg", "s_ckv", "s_kr", "s_ret", "s_hg")}
    for l in range(depth):
        wbig, wuq, wuk, wuv = _prep_layer_weights(w_in[l], w_uq[l], w_uk[l], w_uv[l])
        ckv, kr, q, k, v, zp = _proj_in(x, row2(attn_norm[l]), wbig, row2(q_norm[l]), row2(kv_norm[l]), wuq, wuk,
                                        wuv, row2(lbs[l]), tabs, TOKEN_TILE)
        o_p = _attn_prompt(q, k, v, batch, seq, ATTN_BLOCK)
        o_s = _attn_sample(page_table, q, _bf(jnp.transpose(w_uk[l], (1, 0, 2))), _place_wuv(w_uv[l]), ckv, kr,
                           cache_kv_latent, cache_krt, l, n_p, dec_batch, td, pages_per_step)
        o_mla = jnp.concatenate([o_p, _bf(o_s)], axis=0)
        zp_p = zp.reshape(-1, REC_CHUNK, zp.shape[-1])
        zp_s = jnp.pad(zp[n_p:].reshape(dec_batch, td, -1), ((0, 0), (0, SAMPLE_CHUNK - td), (0, 0)))
        gn_ret, gn_hg = row2(ret_norm[l]), row2(hg_norm[l])
        ret_p, ret_sp = _retention(zp_p, zeros_state, gn_ret, 1, batch, nch, REC_CHUNK)
        ret_s, ret_ss = _retention(zp_s, state_retention[l], gn_ret, SAMPLE_SEQS_PER_STEP, 1, 1, td)
        hg_p, hg_sp = _hgrn(zp_p, zeros_state, gn_hg, 1, batch, nch)
        hg_s, hg_ss = _hgrn(zp_s, state_hgrn[l], gn_hg, SAMPLE_SEQS_PER_STEP, 1, 1)
        join = lambda p, s: jnp.concatenate([p.reshape(n_p, REC_W), s[:, 0, :td].reshape(n_s, REC_W)], axis=0)
        o_ret, o_hg = join(ret_p, ret_s), join(hg_p, hg_s)
        wqh, wql = _split2(peer_wq[l])
        kh, kl = _split2(peer_keys[l].reshape(2 * PEER_HEADS, PEER_NKEYS, PEER_HALF))
        x1, h2t, st = _out_proj(x, o_mla, o_ret, o_hg, _bf(w_out[l]), row2(ffn_norm[l]), wqh, wql, kh, kl,
                                TOKEN_TILE)
        theta, coef, e2 = _peer_select(st)
        x = _peer_experts(h2t, _bf(peer_u[l]), _bf(peer_v[l].T), st.reshape((PEER_HEADS, 2) + st.shape[1:]), e2,
                          theta, coef, x1, row2(final_norm), EXPERT_TOKEN_TILE, EXPERT_ROWS_PER_STEP,
                          l == depth - 1)
        outs["p_ckv"].append(ckv[:n_p].reshape(batch, seq, KV_LORA))
        outs["p_kr"].append(kr[:n_p].reshape(batch, seq, QK_ROPE))
        outs["p_ret"].append(ret_sp)
        outs["p_hg"].append(hg_sp)
        outs["s_ckv"].append(ckv[n_p:].reshape(dec_batch, td, KV_LORA))
        outs["s_kr"].append(kr[n_p:].reshape(dec_batch, td, QK_ROPE))
        outs["s_ret"].append(ret_ss)
        outs["s_hg"].append(hg_ss)
    y_prompt = x[:n_p].reshape(batch, seq, d)
    y_sample = x[n_p:].reshape(dec_batch, td, d)
    return (y_prompt, y_sample) + tuple(jnp.stack(outs[k]) for k in
                                        ("p_ckv", "p_kr", "p_ret", "p_hg", "s_ckv", "s_kr", "s_ret", "s_hg"))
```

```python
import functools
import math

import jax
import jax.numpy as jnp
from jax import lax
from jax.experimental import pallas as pl
from jax.experimental.pallas import tpu as pltpu

D_MODEL = 1024
MLA_HEADS = 8
QK_NOPE = 64
QK_ROPE = 32
V_HEAD = 64
Q_LORA = 256
KV_LORA = 128
RET_HEADS = 4
RET_DK = 64
RET_DV = 64
HG_HEADS = 4
HG_DK = 64
HG_DV = 64
PEER_HEADS = 8
PEER_NKEYS = 128
PEER_HALF = 128
PEER_TOPK = 16
ROPE_BASE = 10000.0
EPS = 1e-6
PAGE_SIZE = 128

LANES = 128
HEAD_SLOT = 128
REC_W = 256
NEG_INF = float("-inf")
VMEM_LIMIT = 56 * 1024 * 1024

ZP_RQ, ZP_RK, ZP_RV, ZP_RG, ZP_GQ, ZP_GF, ZP_GK, ZP_GI, ZP_GG = range(9)
ZP_GROUPS = 9


def _bf(x):
    return x.astype(jnp.bfloat16)


def _dot(a, b):
    return jnp.dot(a, b, preferred_element_type=jnp.float32)


def _dot_nt(a, b):
    return lax.dot_general(a, b, (((1,), (1,)), ((), ())), preferred_element_type=jnp.float32)


def _dot_tn(a, b):
    return lax.dot_general(a, b, (((0,), (0,)), ((), ())), preferred_element_type=jnp.float32)


def _rms(x, g):
    return x * lax.rsqrt(jnp.mean(x * x, axis=-1, keepdims=True) + EPS) * g


def _silu(x):
    return x * (1.0 / (1.0 + jnp.exp(-x)))


def _split3(x):
    x1 = _bf(x)
    r1 = x - x1.astype(jnp.float32)
    x2 = _bf(r1)
    x3 = _bf(r1 - x2.astype(jnp.float32))
    return x1, x2, x3


def _proj_in_kernel(x_ref, gattn_ref, wbig_ref, qn_ref, kvn_ref, wuq_ref, wuk_ref, wuv_ref, lb_ref,
                    cq_ref, sq_ref, c64_ref, s64_ref,
                    ckv_ref, kr_ref, q_ref, k_ref, v_ref, zp_ref):
    scale = (QK_NOPE + QK_ROPE) ** -0.5 * math.log2(math.e)
    h = _bf(_rms(x_ref[...], gattn_ref[...]))
    z = _dot(h, wbig_ref[...])
    o = 0
    cq = z[:, o:o + Q_LORA]; o += Q_LORA
    ckv = z[:, o:o + KV_LORA]; o += KV_LORA
    krp = z[:, o:o + HEAD_SLOT]; o += HEAD_SLOT
    krp_rot = z[:, o:o + HEAD_SLOT]; o += HEAD_SLOT
    groups = []
    for _ in range(10):
        groups.append(z[:, o:o + REC_W]); o += REC_W
    rq, rq_rot, rk, rk_rot, rv, rg, gq, gf, gi, gg = groups

    cq_t = cq_ref[...]
    sq_t = sq_ref[...]
    cqn = _bf(_rms(cq, qn_ref[...]))
    zq = _dot(cqn, wuq_ref[...])
    hw = MLA_HEADS * HEAD_SLOT
    for hd in range(MLA_HEADS):
        a = zq[:, hd * HEAD_SLOT:(hd + 1) * HEAD_SLOT]
        b = zq[:, hw + hd * HEAD_SLOT:hw + (hd + 1) * HEAD_SLOT]
        q_ref[hd] = _bf((a * cq_t + b * sq_t) * scale)
    ckvn = _rms(ckv, kvn_ref[...])
    ckv_ref[...] = ckvn
    kr_placed = krp * cq_t + krp_rot * sq_t
    kr_ref[...] = kr_placed[:, QK_NOPE:QK_NOPE + QK_ROPE]
    ckvb = _bf(ckvn)
    kn = _dot(ckvb, wuk_ref[...])
    for hd in range(MLA_HEADS):
        k_ref[hd] = _bf(kn[:, hd * HEAD_SLOT:(hd + 1) * HEAD_SLOT] + kr_placed)
    vv = _dot(ckvb, wuv_ref[...])
    lane = lax.broadcasted_iota(jnp.int32, (vv.shape[0], HEAD_SLOT), 1)
    for hd in range(MLA_HEADS):
        v_ref[hd] = _bf(jnp.where(lane < V_HEAD, vv[:, hd * HEAD_SLOT:(hd + 1) * HEAD_SLOT], 1.0))

    c64 = c64_ref[...]
    s64 = s64_ref[...]
    zp_ref[:, ZP_RQ * REC_W:(ZP_RQ + 1) * REC_W] = rq * c64 + rq_rot * s64
    zp_ref[:, ZP_RK * REC_W:(ZP_RK + 1) * REC_W] = (rk * c64 + rk_rot * s64) * (RET_DK ** -0.5)
    zp_ref[:, ZP_RV * REC_W:(ZP_RV + 1) * REC_W] = rv
    zp_ref[:, ZP_RG * REC_W:(ZP_RG + 1) * REC_W] = rg
    lb = lb_ref[...]
    sig = 1.0 / (1.0 + jnp.exp(-gf))
    zp_ref[:, ZP_GQ * REC_W:(ZP_GQ + 1) * REC_W] = _silu(gq)
    zp_ref[:, ZP_GF * REC_W:(ZP_GF + 1) * REC_W] = jnp.log(lb + (1.0 - lb) * sig)
    zp_ref[:, ZP_GK * REC_W:(ZP_GK + 1) * REC_W] = (1.0 - lb) * (1.0 / (1.0 + jnp.exp(gf)))
    zp_ref[:, ZP_GI * REC_W:(ZP_GI + 1) * REC_W] = gi
    zp_ref[:, ZP_GG * REC_W:(ZP_GG + 1) * REC_W] = gg


def _rot_cols(w, n_heads, d):
    k = w.shape[0]
    w4 = w.reshape(k, n_heads, 2, d // 2)
    return jnp.concatenate([-w4[:, :, 1:2], w4[:, :, 0:1]], axis=2).reshape(k, n_heads * d)


def _prep_layer_weights(w_in, w_uq, w_uk, w_uv):
    sizes = (Q_LORA, KV_LORA, QK_ROPE) + (REC_W,) * 8
    offs = [0]
    for s in sizes:
        offs.append(offs[-1] + s)
    cols = [w_in[:, offs[i]:offs[i + 1]] for i in range(len(sizes))]
    cq, ckv, kr, rq, rk, rv, rg, gq, gf, gi, gg = cols
    kdim = w_in.shape[0]
    zeros = lambda n: jnp.zeros((kdim, n), w_in.dtype)
    place = lambda c: jnp.concatenate([zeros(QK_NOPE), c, zeros(HEAD_SLOT - QK_NOPE - QK_ROPE)], axis=1)
    wbig = jnp.concatenate([
        cq, ckv, place(kr), place(_rot_cols(kr, 1, QK_ROPE)),
        rq, _rot_cols(rq, RET_HEADS, RET_DK), rk, _rot_cols(rk, RET_HEADS, RET_DK),
        rv, rg, gq, gf, gi, gg], axis=1)
    wq3 = w_uq.reshape(Q_LORA, MLA_HEADS, QK_NOPE + QK_ROPE)
    nope, ropep = wq3[..., :QK_NOPE], wq3[..., QK_NOPE:]
    rope_rot = jnp.concatenate([-ropep[..., QK_ROPE // 2:], ropep[..., :QK_ROPE // 2]], axis=-1)
    pad = jnp.zeros((Q_LORA, MLA_HEADS, HEAD_SLOT - QK_NOPE - QK_ROPE), w_uq.dtype)
    plain = jnp.concatenate([nope, ropep, pad], axis=-1).reshape(Q_LORA, MLA_HEADS * HEAD_SLOT)
    rotd = jnp.concatenate([jnp.zeros_like(nope), rope_rot, pad], axis=-1).reshape(Q_LORA, MLA_HEADS * HEAD_SLOT)
    wuq = jnp.concatenate([plain, rotd], axis=1)
    wuk = jnp.concatenate([w_uk, jnp.zeros((KV_LORA, MLA_HEADS, HEAD_SLOT - QK_NOPE), w_uk.dtype)],
                          axis=-1).reshape(KV_LORA, MLA_HEADS * HEAD_SLOT)
    wuv = jnp.concatenate([w_uv, jnp.zeros((KV_LORA, MLA_HEADS, HEAD_SLOT - V_HEAD), w_uv.dtype)],
                          axis=-1).reshape(KV_LORA, MLA_HEADS * HEAD_SLOT)
    return _bf(wbig), _bf(wuq), _bf(wuk), _bf(wuv)


def _rope_tables(pos):
    posf = pos.astype(jnp.float32)[:, None]

    def cs(d):
        half = d // 2
        inv = ROPE_BASE ** (-jnp.arange(half, dtype=jnp.float32) / half)
        ang = posf * inv[None, :]
        return jnp.cos(ang), jnp.sin(ang)

    n = pos.shape[0]
    c32, s32 = cs(QK_ROPE)
    tail = jnp.zeros((n, HEAD_SLOT - QK_NOPE - QK_ROPE), jnp.float32)
    cq = jnp.concatenate([jnp.ones((n, QK_NOPE), jnp.float32), c32, c32, tail], axis=1)
    sq = jnp.concatenate([jnp.zeros((n, QK_NOPE), jnp.float32), s32, s32, tail], axis=1)
    c64, s64 = cs(RET_DK)
    c64 = jnp.tile(jnp.concatenate([c64, c64], axis=1), (1, RET_HEADS))
    s64 = jnp.tile(jnp.concatenate([s64, s64], axis=1), (1, RET_HEADS))
    return cq, sq, c64, s64


def _proj_in(x, gattn, wbig, qn, kvn, wuq, wuk, wuv, lb, tabs, tm):
    n = x.shape[0]
    cq_t, sq_t, c64, s64 = tabs
    full = lambda a: pl.BlockSpec(a.shape, lambda i: (0,) * a.ndim)
    row = lambda w: pl.BlockSpec((tm, w), lambda i: (i, 0))
    hrow = lambda nh: pl.BlockSpec((nh, tm, HEAD_SLOT), lambda i: (0, i, 0))
    out_shape = (
        jax.ShapeDtypeStruct((n, KV_LORA), jnp.float32),
        jax.ShapeDtypeStruct((n, QK_ROPE), jnp.float32),
        jax.ShapeDtypeStruct((MLA_HEADS, n, HEAD_SLOT), jnp.bfloat16),
        jax.ShapeDtypeStruct((MLA_HEADS, n, HEAD_SLOT), jnp.bfloat16),
        jax.ShapeDtypeStruct((MLA_HEADS, n, HEAD_SLOT), jnp.bfloat16),
        jax.ShapeDtypeStruct((n, ZP_GROUPS * REC_W), jnp.float32),
    )
    return pl.pallas_call(
        _proj_in_kernel,
        grid=(n // tm,),
        in_specs=[row(D_MODEL), full(gattn), full(wbig), full(qn), full(kvn), full(wuq), full(wuk), full(wuv),
                  full(lb), row(HEAD_SLOT), row(HEAD_SLOT), row(REC_W), row(REC_W)],
        out_specs=(row(KV_LORA), row(QK_ROPE), hrow(MLA_HEADS), hrow(MLA_HEADS), hrow(MLA_HEADS),
                   row(ZP_GROUPS * REC_W)),
        out_shape=out_shape,
        compiler_params=pltpu.CompilerParams(dimension_semantics=("parallel",), vmem_limit_bytes=VMEM_LIMIT),
        name="proj_in",
    )(x, gattn, wbig, qn, kvn, wuq, wuk, wuv, lb, cq_t, sq_t, c64, s64)


def _online_step(s, vb, m, acc):
    m_new = jnp.maximum(m, jnp.max(s, axis=-1, keepdims=True))
    acc = jnp.exp2(m - m_new) * acc + _dot(_bf(jnp.exp2(s - m_new)), vb)
    return m_new, acc


def _attn_prompt_kernel(q_ref, k_ref, v_ref, o_ref, *, blk):
    qi = pl.program_id(2)
    qs = (q_ref[0], q_ref[1])

    def step(j, carry, masked):
        start = pl.multiple_of(j * blk, blk)
        out = []
        for hh in range(2):
            s = _dot_nt(qs[hh], k_ref[hh, pl.ds(start, blk), :])
            if masked:
                row = lax.broadcasted_iota(jnp.int32, (blk, blk), 0)
                col = lax.broadcasted_iota(jnp.int32, (blk, blk), 1)
                s = jnp.where(col <= row, s, NEG_INF)
            out.append(_online_step(s, v_ref[hh, pl.ds(start, blk), :], *carry[hh]))
        return tuple(out)

    init = (jnp.full((blk, 1), NEG_INF, jnp.float32), jnp.zeros((blk, LANES), jnp.float32))
    pairs = qi // 2
    carry = lax.fori_loop(0, pairs, lambda j, c: step(2 * j + 1, step(2 * j, c, False), False), (init, init))
    carry = lax.fori_loop(2 * pairs, qi, lambda j, c: step(j, c, False), carry)
    carry = step(qi, carry, True)
    norm = [acc / pltpu.roll(acc, V_HEAD, axis=1) for _, acc in carry]
    lane = lax.broadcasted_iota(jnp.int32, (blk, LANES), 1)
    o_ref[...] = _bf(jnp.where(lane < V_HEAD, norm[0], pltpu.roll(norm[1], V_HEAD, axis=1)))


def _attn_prompt(q, k, v, batch, seq, blk):
    nq = seq // blk
    return pl.pallas_call(
        functools.partial(_attn_prompt_kernel, blk=blk),
        grid=(batch, MLA_HEADS // 2, nq),
        in_specs=[pl.BlockSpec((2, blk, HEAD_SLOT), lambda b, p, i: (p, b * nq + i, 0)),
                  pl.BlockSpec((2, seq, HEAD_SLOT), lambda b, p, i: (p, b, 0)),
                  pl.BlockSpec((2, seq, LANES), lambda b, p, i: (p, b, 0))],
        out_specs=pl.BlockSpec((blk, LANES), lambda b, p, i: (b * nq + i, p)),
        out_shape=jax.ShapeDtypeStruct((batch * seq, MLA_HEADS * V_HEAD), jnp.bfloat16),
        compiler_params=pltpu.CompilerParams(dimension_semantics=("parallel", "parallel", "arbitrary"),
                                             vmem_limit_bytes=VMEM_LIMIT),
        name="attn_prompt",
    )(q, k, v)


def _attn_sample_kernel(pt_ref, q_ref, wuk_ref, wuvp_ref, ckv_ref, kr_ref, *rest, pages_per_step, td):
    kv_refs = rest[:pages_per_step]
    krt_refs = rest[pages_per_step:2 * pages_per_step]
    o_ref = rest[2 * pages_per_step]
    qlat_ref, qrope_ref, m_ref, l_ref, acc_ref, kall_ref, krt_ref = rest[2 * pages_per_step + 1:]
    j = pl.program_id(1)
    rows = MLA_HEADS * td

    @pl.when(j == 0)
    def _():
        for hd in range(MLA_HEADS):
            qh = q_ref[hd].astype(jnp.float32)
            qlat_ref[hd * td:(hd + 1) * td, :] = _dot_nt(_bf(qh[:, :QK_NOPE]), wuk_ref[hd])
            qrope_ref[hd * td:(hd + 1) * td, :] = qh[:, QK_NOPE:QK_NOPE + QK_ROPE]
        m_ref[...] = jnp.full((rows, 1), NEG_INF, jnp.float32)
        l_ref[...] = jnp.zeros((rows, 1), jnp.float32)
        acc_ref[...] = jnp.zeros((rows, KV_LORA), jnp.float32)

    for r in range(pages_per_step):
        kall_ref[r * PAGE_SIZE:(r + 1) * PAGE_SIZE, :] = _bf(kv_refs[r][...])
        krt_ref[:, r * PAGE_SIZE:(r + 1) * PAGE_SIZE] = _bf(krt_refs[r][...])
    qlat = _bf(qlat_ref[...])
    qrope = _bf(qrope_ref[...])
    kall = kall_ref[...]
    s = _dot_nt(qlat, kall) + _dot(qrope, krt_ref[...])
    m = m_ref[...]
    m_new = jnp.maximum(m, jnp.max(s, axis=-1, keepdims=True))
    alpha = jnp.exp2(m - m_new)
    p = jnp.exp2(s - m_new)
    l = alpha * l_ref[...] + jnp.sum(p, axis=-1, keepdims=True)
    acc = alpha * acc_ref[...] + _dot(_bf(p), kall)
    m_ref[...], l_ref[...], acc_ref[...] = m_new, l, acc

    @pl.when(j == pl.num_programs(1) - 1)
    def _():
        pad = jnp.zeros((PAGE_SIZE - td, KV_LORA), jnp.float32)
        cnew = _bf(jnp.concatenate([ckv_ref[...], pad], axis=0))
        krnew = _bf(jnp.concatenate([kr_ref[...], pad[:, :QK_ROPE]], axis=0))
        sn = _dot_nt(qlat, cnew) + _dot_nt(qrope, krnew)
        row = lax.broadcasted_iota(jnp.int32, (rows, PAGE_SIZE), 0)
        col = lax.broadcasted_iota(jnp.int32, (rows, PAGE_SIZE), 1)
        sn = jnp.where(col <= row % td, sn, NEG_INF)
        m2 = jnp.maximum(m_new, jnp.max(sn, axis=-1, keepdims=True))
        a2 = jnp.exp2(m_new - m2)
        pn = jnp.exp2(sn - m2)
        l2 = a2 * l + jnp.sum(pn, axis=-1, keepdims=True)
        acc2 = a2 * acc + _dot(_bf(pn), cnew)
        olat = _bf(acc2 / l2)
        out = jnp.zeros((td, MLA_HEADS * V_HEAD), jnp.float32)
        for hd in range(MLA_HEADS):
            out = out + _dot(olat[hd * td:(hd + 1) * td, :], wuvp_ref[hd])
        o_ref[...] = out


def _attn_sample(page_table, q, wuk_h, wuv_placed, ckv, kr, cache_kv, cache_krt, layer, row0, dec_batch, td,
                 pages_per_step):
    n_pages = page_table.shape[1]
    steps = n_pages // pages_per_step
    blk0 = row0 // td
    rows = MLA_HEADS * td
    keys = pages_per_step * PAGE_SIZE

    def page_spec(shape, r):
        return pl.BlockSpec((None, None) + shape,
                            lambda b, j, pt, r=r: (layer, pt[b * n_pages + j * pages_per_step + r], 0, 0))

    in_specs = [pl.BlockSpec((MLA_HEADS, td, HEAD_SLOT), lambda b, j, pt: (0, blk0 + b, 0)),
                pl.BlockSpec(wuk_h.shape, lambda b, j, pt: (0, 0, 0)),
                pl.BlockSpec(wuv_placed.shape, lambda b, j, pt: (0, 0, 0)),
                pl.BlockSpec((td, KV_LORA), lambda b, j, pt: (blk0 + b, 0)),
                pl.BlockSpec((td, QK_ROPE), lambda b, j, pt: (blk0 + b, 0))]
    in_specs += [page_spec((PAGE_SIZE, KV_LORA), r) for r in range(pages_per_step)]
    in_specs += [page_spec((QK_ROPE, PAGE_SIZE), r) for r in range(pages_per_step)]
    grid_spec = pltpu.PrefetchScalarGridSpec(
        num_scalar_prefetch=1,
        grid=(dec_batch, steps),
        in_specs=in_specs,
        out_specs=pl.BlockSpec((td, MLA_HEADS * V_HEAD), lambda b, j, pt: (b, 0)),
        scratch_shapes=[pltpu.VMEM((rows, KV_LORA), jnp.float32), pltpu.VMEM((rows, QK_ROPE), jnp.float32),
                        pltpu.VMEM((rows, 1), jnp.float32), pltpu.VMEM((rows, 1), jnp.float32),
                        pltpu.VMEM((rows, KV_LORA), jnp.float32),
                        pltpu.VMEM((keys, KV_LORA), jnp.bfloat16), pltpu.VMEM((QK_ROPE, keys), jnp.bfloat16)])
    return pl.pallas_call(
        functools.partial(_attn_sample_kernel, pages_per_step=pages_per_step, td=td),
        grid_spec=grid_spec,
        out_shape=jax.ShapeDtypeStruct((dec_batch * td, MLA_HEADS * V_HEAD), jnp.float32),
        compiler_params=pltpu.CompilerParams(dimension_semantics=("parallel", "arbitrary"),
                                             vmem_limit_bytes=VMEM_LIMIT),
        name="attn_sample",
    )(page_table.reshape(-1), q, wuk_h, wuv_placed, ckv, kr, *([cache_kv] * pages_per_step),
      *([cache_krt] * pages_per_step))


def _place_wuv(w_uv):
    eye = jnp.eye(MLA_HEADS, dtype=w_uv.dtype)
    placed = jnp.einsum('chv,hg->hcgv', w_uv, eye)
    return _bf(placed.reshape(MLA_HEADS, KV_LORA, MLA_HEADS * V_HEAD))


def _ret_gammas():
    return [1.0 - 2.0 ** (-5.0 - h) for h in range(RET_HEADS)]


def _ret_kernel(*refs, nb, parts, decay_len):
    q_refs, k_refs, v_refs, g_refs = (refs[g * parts:(g + 1) * parts] for g in range(4))
    s0_ref, dmask_ref, qdec_ref, kdec_ref, gn_ref, o_ref, sfin_ref, s_ref = refs[4 * parts:]
    c = pl.program_id(1)

    @pl.when(c == 0)
    def _():
        s_ref[...] = s0_ref[...]

    qdec = qdec_ref[...]
    kdec = kdec_ref[...]
    gn = gn_ref[...]
    for sq in range(parts * nb):
        p, bi = divmod(sq, nb)
        q = q_refs[p][bi]
        k = k_refs[p][bi]
        v = v_refs[p][bi]
        kd = k * kdec
        gate = _silu(g_refs[p][bi])
        for h in range(RET_HEADS):
            sl = slice(h * RET_DK, (h + 1) * RET_DK)
            qh, kh, vh = _bf(q[:, sl]), _bf(k[:, sl]), _bf(v[:, sl])
            a = _dot_nt(qh, kh) * dmask_ref[h]
            s = s_ref[sq, h]
            o = _dot(_bf(a), vh) + _dot(qh, _bf(s)) * qdec[:, sl]
            s_ref[sq, h] = s * (_ret_gammas()[h] ** decay_len) + _dot_tn(_bf(kd[:, sl]), vh)
            oc = o - jnp.mean(o, axis=-1, keepdims=True)
            y = oc * lax.rsqrt(jnp.mean(oc * oc, axis=-1, keepdims=True) + EPS)
            o_ref[sq, :, sl] = _bf(y * gn[:, sl] * gate[:, sl])

    @pl.when(c == pl.num_programs(1) - 1)
    def _():
        sfin_ref[...] = s_ref[...]


def _ret_consts(chunk, decay_len):
    lg = jnp.log(jnp.asarray(_ret_gammas(), jnp.float32))
    idx = jnp.arange(chunk, dtype=jnp.float32)
    diff = idx[:, None] - idx[None, :]
    dmask = jnp.where(diff >= 0, jnp.exp(lg[:, None, None] * jnp.maximum(diff, 0.0)), 0.0)
    qdec = jnp.exp(lg[None, :] * (idx[:, None] + 1.0))
    kdec = jnp.exp(lg[None, :] * (decay_len - 1.0 - idx[:, None]))
    rep = lambda a: jnp.repeat(a, RET_DK, axis=1)
    return dmask, rep(qdec), rep(kdec)


def _rec_specs(nb, parts, chunk, nch, groups):
    return [pl.BlockSpec((nb, chunk, REC_W), lambda b, c, g=g, p=p: ((b * parts + p) * nch + c, 0, g))
            for g in groups for p in range(parts)]


def _rec_call(kernel_fn, name, groups, zp3, s0, extra, nb, parts, nch):
    chunk = zp3.shape[1]
    seqs = s0.shape[0]
    per_step = parts * nb
    full = lambda a: pl.BlockSpec(a.shape, lambda b, c: (0,) * a.ndim)
    st = pl.BlockSpec((per_step,) + s0.shape[1:], lambda b, c: (b, 0, 0, 0))
    return pl.pallas_call(
        kernel_fn,
        grid=(seqs // per_step, nch),
        in_specs=_rec_specs(nb, parts, chunk, nch, groups) + [st] + [full(a) for a in extra],
        out_specs=(pl.BlockSpec((per_step, None, chunk, REC_W), lambda b, c: (b, c, 0, 0)), st),
        out_shape=(jax.ShapeDtypeStruct((seqs, nch, chunk, REC_W), jnp.bfloat16),
                   jax.ShapeDtypeStruct(s0.shape, jnp.float32)),
        scratch_shapes=[pltpu.VMEM((per_step,) + s0.shape[1:], jnp.float32)],
        compiler_params=pltpu.CompilerParams(dimension_semantics=("parallel", "arbitrary"),
                                             vmem_limit_bytes=VMEM_LIMIT),
        name=name,
    )(*([zp3] * (len(groups) * parts)), s0, *extra)


def _retention(zp3, s0, gn, nb, parts, nch, decay_len):
    consts = _ret_consts(zp3.shape[1], decay_len)
    return _rec_call(functools.partial(_ret_kernel, nb=nb, parts=parts, decay_len=decay_len), "retention",
                     (ZP_RQ, ZP_RK, ZP_RV, ZP_RG), zp3, s0, consts + (gn,), nb, parts, nch)


def _hgrn_sum_matrix(chunk):
    import numpy as np
    t = np.arange(chunk)[:, None]
    r = np.arange(chunk)[None, :]
    blocks = [r <= t, r > t]
    m = 2
    while m <= chunk:
        half = m // 2
        mid = (t // m) * m + half
        upper = (t % m) >= half
        blocks.append(upper & (r >= mid) & (r <= t))
        blocks.append(~upper & (r > t) & (r <= mid - 1))
        m *= 2
    return jnp.asarray(np.concatenate(blocks, axis=0).astype(np.float32), jnp.bfloat16)


def _hgrn_kernel(*refs, nb, parts, chunk):
    q_refs, f_refs, k_refs, v_refs, g_refs = (refs[g * parts:(g + 1) * parts] for g in range(5))
    s0_ref, msum_ref, gn_ref, o_ref, sfin_ref, st_ref = refs[5 * parts:]
    c = pl.program_id(1)
    nlev = chunk.bit_length() - 1

    @pl.when(c == 0)
    def _():
        for bi in range(parts * nb):
            for h in range(HG_HEADS):
                st_ref[bi, h] = s0_ref[bi, h].T

    gn = gn_ref[...]
    rowi = lax.broadcasted_iota(jnp.int32, (chunk, REC_W), 0)
    r2 = lax.broadcasted_iota(jnp.int32, (chunk, chunk), 0)
    c2 = lax.broadcasted_iota(jnp.int32, (chunk, chunk), 1)
    for sq in range(parts * nb):
        p, bi = divmod(sq, nb)
        q = q_refs[p][bi]
        k = k_refs[p][bi]
        v = v_refs[p][bi]
        gate = _silu(g_refs[p][bi])
        f1, f2, f3 = _split3(f_refs[p][bi])
        sums = _dot(msum_ref[...], jnp.concatenate([f1, f2, f3], axis=1))
        sums = sums[:, :REC_W] + sums[:, REC_W:2 * REC_W] + sums[:, 2 * REC_W:]
        part = lambda i: sums[i * chunk:(i + 1) * chunk]
        b = part(0)
        qb = _bf(q * jnp.exp(b))
        ks = _bf(k * jnp.exp(part(1)))
        e_last = jnp.exp(b[chunk - 1:chunk, :])
        vb = _bf(v)
        qs, kls = [_bf(q)], [_bf(k)]
        for lev in range(1, nlev + 1):
            upper = (rowi & (1 << (lev - 1))) != 0
            qs.append(_bf(jnp.where(upper, q * jnp.exp(part(2 * lev)), 0.0)))
            kls.append(_bf(jnp.where(upper, 0.0, k * jnp.exp(part(2 * lev + 1)))))
        for h in range(HG_HEADS):
            sl = slice(h * HG_DK, (h + 1) * HG_DK)
            a = jnp.where(r2 == c2, _dot_nt(qs[0][:, sl], kls[0][:, sl]), 0.0)
            for lev in range(1, nlev + 1):
                same = (r2 >> lev) == (c2 >> lev)
                a = a + jnp.where(same, _dot_nt(qs[lev][:, sl], kls[lev][:, sl]), 0.0)
            st = st_ref[sq, h]
            o = _dot(_bf(a), vb[:, sl]) + _dot_nt(qb[:, sl], _bf(st))
            st_ref[sq, h] = st * e_last[:, sl] + _dot_tn(vb[:, sl], ks[:, sl])
            y = o * lax.rsqrt(jnp.mean(o * o, axis=-1, keepdims=True) + EPS)
            o_ref[sq, :, sl] = _bf(y * gn[:, sl] * gate[:, sl])

    @pl.when(c == pl.num_programs(1) - 1)
    def _():
        for bi in range(parts * nb):
            for h in range(HG_HEADS):
                sfin_ref[bi, h] = st_ref[bi, h].T


def _hgrn(zp3, s0, gn, nb, parts, nch):
    chunk = zp3.shape[1]
    return _rec_call(functools.partial(_hgrn_kernel, nb=nb, parts=parts, chunk=chunk), "hgrn2",
                     (ZP_GQ, ZP_GF, ZP_GK, ZP_GI, ZP_GG), zp3, s0, (_hgrn_sum_matrix(chunk), gn), nb, parts, nch)


def _dot3(ah, al, bh, bl, dot):
    return dot(ah, bh) + dot(al, bh) + dot(ah, bl)


def _split2(x):
    hi = _bf(x)
    return hi, _bf(x - hi.astype(jnp.float32))


def _out_proj_kernel(x_ref, omla_ref, oret_ref, ohg_ref, wout_ref, gffn_ref, wqh_ref, wql_ref, kh_ref, kl_ref,
                     x1_ref, h2t_ref, st_ref):
    mix = jnp.concatenate([omla_ref[...], oret_ref[...], ohg_ref[...]], axis=1)
    x1 = x_ref[...] + _dot(mix, wout_ref[...])
    x1_ref[...] = x1
    h2 = _rms(x1, gffn_ref[...])
    h2t_ref[...] = _bf(h2.T)
    hh, hl = _split2(h2)
    q = _dot3(hh, hl, wqh_ref[...], wql_ref[...], _dot)
    for p in range(2 * PEER_HEADS):
        qh, ql = _split2(q[:, p * PEER_HALF:(p + 1) * PEER_HALF])
        sp = _dot3(kh_ref[p], kl_ref[p], qh, ql, _dot_nt)
        for c in range(sp.shape[1] // LANES):
            st_ref[p, c] = sp[:, c * LANES:(c + 1) * LANES]


def _out_proj(x, omla, oret, ohg, wout, gffn, wqh, wql, kh, kl, tm):
    n = x.shape[0]
    full = lambda a: pl.BlockSpec(a.shape, lambda i: (0,) * a.ndim)
    row = lambda w: pl.BlockSpec((tm, w), lambda i: (i, 0))
    return pl.pallas_call(
        _out_proj_kernel,
        grid=(n // tm,),
        in_specs=[row(D_MODEL), row(MLA_HEADS * V_HEAD), row(REC_W), row(REC_W), full(wout), full(gffn), full(wqh),
                  full(wql), full(kh), full(kl)],
        out_specs=(row(D_MODEL), pl.BlockSpec((D_MODEL, tm), lambda i: (0, i)),
                   pl.BlockSpec((2 * PEER_HEADS, tm // LANES, PEER_NKEYS, LANES), lambda i: (0, i, 0, 0))),
        out_shape=(jax.ShapeDtypeStruct((n, D_MODEL), jnp.float32),
                   jax.ShapeDtypeStruct((D_MODEL, n), jnp.bfloat16),
                   jax.ShapeDtypeStruct((2 * PEER_HEADS, n // LANES, PEER_NKEYS, LANES), jnp.float32)),
        compiler_params=pltpu.CompilerParams(dimension_semantics=("parallel",), vmem_limit_bytes=VMEM_LIMIT),
        name="out_proj_peer_scores",
    )(x, omla, oret, ohg, wout, gffn, wqh, wql, kh, kl)


def _extract_top(x, count, on_value):
    rowf = lax.broadcasted_iota(jnp.int32, x.shape, 1).astype(jnp.float32)
    for r in range(count):
        m = jnp.max(x, axis=1, keepdims=True)
        first = jnp.min(jnp.where(x == m, rowf, float(x.shape[1])), axis=1, keepdims=True)
        hit = rowf == first
        on_value(r, m, hit)
        x = jnp.where(hit, NEG_INF, x)
    return x


SUB = 8


def _peer_select_kernel(s_ref, theta_ref, coef_ref, e2_ref, v1_ref, v2_ref, cand_ref):
    s1 = s_ref[:, 0]
    s2 = s_ref[:, 1]

    def keep(ref):
        def on_value(r, m, hit):
            ref[:, r:r + 1, :] = m
        return on_value

    _extract_top(s1, PEER_TOPK + 1, keep(v1_ref))
    _extract_top(s2, PEER_TOPK + 1, keep(v2_ref))
    for r1 in range(SUB):
        cand_ref[:, r1 * SUB:(r1 + 1) * SUB, :] = v1_ref[:, r1:r1 + 1, :] + v2_ref[:, 0:SUB, :]
    cand_ref[:, SUB * SUB:SUB * SUB + SUB, :] = v1_ref[:, 0:1, :] + v2_ref[:, SUB:2 * SUB, :]
    cand_ref[:, SUB * SUB + SUB:SUB * SUB + 2 * SUB, :] = v1_ref[:, SUB:2 * SUB, :] + v2_ref[:, 0:1, :]
    best = v1_ref[:, 0:1, :] + v2_ref[:, 0:1, :]
    picked = []
    rest = _extract_top(cand_ref[...], PEER_TOPK, lambda r, m, hit: picked.append(m))
    z = jnp.zeros_like(best)
    for m in picked:
        z = z + jnp.exp(m - best)
    nxt = jnp.max(rest, axis=1, keepdims=True)
    nxt = jnp.maximum(nxt, v1_ref[:, PEER_TOPK:PEER_TOPK + 1, :] + v2_ref[:, 0:1, :])
    nxt = jnp.maximum(nxt, v1_ref[:, 0:1, :] + v2_ref[:, PEER_TOPK:PEER_TOPK + 1, :])
    tau = 0.5 * (picked[-1] + nxt)
    theta_ref[...] = tau - s1
    coef_ref[...] = jnp.exp(s1 - v1_ref[:, 0:1, :]) * (1.0 / z)
    e2_ref[...] = jnp.exp(s2 - v2_ref[:, 0:1, :])


def _peer_select(st):
    nt = st.shape[1]
    s5 = st.reshape(PEER_HEADS, 2, nt, PEER_NKEYS, LANES)
    out = pl.BlockSpec((PEER_HEADS, None, PEER_NKEYS, LANES), lambda t: (0, t, 0, 0))
    shape = jax.ShapeDtypeStruct((PEER_HEADS, nt, PEER_NKEYS, LANES), jnp.float32)
    vrows = SUB * ((PEER_TOPK + 1 + SUB - 1) // SUB)
    return pl.pallas_call(
        _peer_select_kernel,
        grid=(nt,),
        in_specs=[pl.BlockSpec((PEER_HEADS, 2, None, PEER_NKEYS, LANES), lambda t: (0, 0, t, 0, 0))],
        out_specs=(out, out, out),
        out_shape=(shape, shape, shape),
        scratch_shapes=[pltpu.VMEM((PEER_HEADS, vrows, LANES), jnp.float32),
                        pltpu.VMEM((PEER_HEADS, vrows, LANES), jnp.float32),
                        pltpu.VMEM((PEER_HEADS, SUB * SUB + 2 * SUB, LANES), jnp.float32)],
        compiler_params=pltpu.CompilerParams(dimension_semantics=("parallel",), vmem_limit_bytes=VMEM_LIMIT),
        name="peer_select",
    )(s5)


def _peer_expert_kernel(h2t_ref, u_ref, vt_ref, s2_ref, e2_ref, theta_ref, coef_ref, x1_ref, gfin_ref, o_ref,
                        acc_ref, p_ref, *, ni, final_norm):
    step = pl.program_id(1)

    @pl.when(step == 0)
    def _():
        acc_ref[...] = jnp.zeros_like(acc_ref)

    h2t = h2t_ref[...]
    tm = h2t.shape[1]
    for g in range(ni // 2):
        at = _dot(u_ref[g * 2 * PEER_NKEYS:(g + 1) * 2 * PEER_NKEYS, :], h2t)
        for half in range(2):
            ib = 2 * g + half
            i = step * ni + ib
            for c in range(tm // LANES):
                cs = slice(c * LANES, (c + 1) * LANES)
                w = None
                for h in range(PEER_HEADS):
                    th = theta_ref[h, c, pl.ds(i, 1), :]
                    cf = coef_ref[h, c, pl.ds(i, 1), :]
                    term = jnp.where(s2_ref[h, c] >= th, e2_ref[h, c] * cf, 0.0)
                    w = term if w is None else w + term
                a = at[half * PEER_NKEYS:(half + 1) * PEER_NKEYS, cs]
                gelu = 0.5 * a * (1.0 + lax.erf(a * (2.0 ** -0.5)))
                p_ref[ib * PEER_NKEYS:(ib + 1) * PEER_NKEYS, cs] = _bf(gelu * w)
    acc_ref[...] += _dot(vt_ref[...], p_ref[...])

    @pl.when(step == pl.num_programs(1) - 1)
    def _():
        x2 = x1_ref[...] + acc_ref[...].T
        o_ref[...] = _rms(x2, gfin_ref[...]) if final_norm else x2


def _peer_experts(h2t, u, vt, s5, e2, theta, coef, x1, gfin, tm, ni, final_norm):
    n = x1.shape[0]
    et = ni * PEER_NKEYS
    blocks = PEER_NKEYS // ni
    per_head = pl.BlockSpec((PEER_HEADS, tm // LANES, PEER_NKEYS, LANES), lambda t, i: (0, t, 0, 0))
    return pl.pallas_call(
        functools.partial(_peer_expert_kernel, ni=ni, final_norm=final_norm),
        grid=(n // tm, blocks),
        in_specs=[pl.BlockSpec((D_MODEL, tm), lambda t, i: (0, t)),
                  pl.BlockSpec((et, D_MODEL), lambda t, i: (i, 0)),
                  pl.BlockSpec((None, D_MODEL, et), lambda t, i: (i, 0, 0)),
                  pl.BlockSpec((PEER_HEADS, None, tm // LANES, PEER_NKEYS, LANES), lambda t, i: (0, 1, t, 0, 0)),
                  per_head, per_head, per_head,
                  pl.BlockSpec((tm, D_MODEL), lambda t, i: (t, 0)),
                  pl.BlockSpec(gfin.shape, lambda t, i: (0, 0))],
        out_specs=pl.BlockSpec((tm, D_MODEL), lambda t, i: (t, 0)),
        out_shape=jax.ShapeDtypeStruct((n, D_MODEL), jnp.float32),
        scratch_shapes=[pltpu.VMEM((D_MODEL, tm), jnp.float32), pltpu.VMEM((et, tm), jnp.bfloat16)],
        compiler_params=pltpu.CompilerParams(dimension_semantics=("parallel", "arbitrary"),
                                             vmem_limit_bytes=VMEM_LIMIT),
        name="peer_experts",
    )(h2t, u, vt, s5, e2, theta, coef, x1, gfin)


TOKEN_TILE = 256
ATTN_BLOCK = 512
REC_CHUNK = 128
SAMPLE_CHUNK = 16
SAMPLE_SEQS_PER_STEP = 8
PAGES_PER_STEP = 64
EXPERT_TOKEN_TILE = 512
EXPERT_ROWS_PER_STEP = 8


def kernel(x_prompt, x_sample, cache_kv_latent, cache_k_rope, state_retention, state_hgrn, page_table,
           w_in, q_norm, w_uq, kv_norm, w_uk, w_uv, ret_norm, hg_norm, hg_lower_bounds, w_out,
           attn_norm, ffn_norm, final_norm, peer_wq, peer_keys, peer_u, peer_v):
    batch, seq, d = x_prompt.shape
    dec_batch, td, _ = x_sample.shape
    depth = w_in.shape[0]
    n_p, n_s = batch * seq, dec_batch * td
    past_len = page_table.shape[1] * PAGE_SIZE
    f32 = jnp.float32

    x = jnp.concatenate([x_prompt.reshape(n_p, d), x_sample.reshape(n_s, d)], axis=0)
    pos = jnp.concatenate([jnp.tile(jnp.arange(seq, dtype=jnp.int32), batch),
                           jnp.tile(past_len + jnp.arange(td, dtype=jnp.int32), dec_batch)])
    tabs = _rope_tables(pos)
    cache_krt = jnp.swapaxes(cache_k_rope, 2, 3)
    pages_per_step = math.gcd(PAGES_PER_STEP, page_table.shape[1])
    lb_soft = jax.nn.softmax(hg_lower_bounds.astype(f32), axis=0)
    lbs = jnp.cumsum(lb_soft, axis=0) - lb_soft[0]
    zeros_state = jnp.zeros((batch, RET_HEADS, RET_DK, RET_DV), f32)
    nch = seq // REC_CHUNK
    row2 = lambda a: a.reshape(1, -1)

    outs = {k: [] for k in ("p_ckv", "p_kr", "p_ret", "p_hg", "s_ckv", "s_kr", "s_ret", "s_hg")}
    for l in range(depth):
        wbig, wuq, wuk, wuv = _prep_layer_weights(w_in[l], w_uq[l], w_uk[l], w_uv[l])
        ckv, kr, q, k, v, zp = _proj_in(x, row2(attn_norm[l]), wbig, row2(q_norm[l]), row2(kv_norm[l]), wuq, wuk,
                                        wuv, row2(lbs[l]), tabs, TOKEN_TILE)
        o_p = _attn_prompt(q, k, v, batch, seq, ATTN_BLOCK)
        o_s = _attn_sample(page_table, q, _bf(jnp.transpose(w_uk[l], (1, 0, 2))), _place_wuv(w_uv[l]), ckv, kr,
                           cache_kv_latent, cache_krt, l, n_p, dec_batch, td, pages_per_step)
        o_mla = jnp.concatenate([o_p, _bf(o_s)], axis=0)
        zp_p = zp.reshape(-1, REC_CHUNK, zp.shape[-1])
        zp_s = jnp.pad(zp[n_p:].reshape(dec_batch, td, -1), ((0, 0), (0, SAMPLE_CHUNK - td), (0, 0)))
        gn_ret, gn_hg = row2(ret_norm[l]), row2(hg_norm[l])
        ret_p, ret_sp = _retention(zp_p, zeros_state, gn_ret, 1, batch, nch, REC_CHUNK)
        ret_s, ret_ss = _retention(zp_s, state_retention[l], gn_ret, SAMPLE_SEQS_PER_STEP, 1, 1, td)
        hg_p, hg_sp = _hgrn(zp_p, zeros_state, gn_hg, 1, batch, nch)
        hg_s, hg_ss = _hgrn(zp_s, state_hgrn[l], gn_hg, SAMPLE_SEQS_PER_STEP, 1, 1)
        join = lambda p, s: jnp.concatenate([p.reshape(n_p, REC_W), s[:, 0, :td].reshape(n_s, REC_W)], axis=0)
        o_ret, o_hg = join(ret_p, ret_s), join(hg_p, hg_s)
        wqh, wql = _split2(peer_wq[l])
        kh, kl = _split2(peer_keys[l].reshape(2 * PEER_HEADS, PEER_NKEYS, PEER_HALF))
        x1, h2t, st = _out_proj(x, o_mla, o_ret, o_hg, _bf(w_out[l]), row2(ffn_norm[l]), wqh, wql, kh, kl,
                                TOKEN_TILE)
        theta, coef, e2 = _peer_select(st)
        et = EXPERT_ROWS_PER_STEP * PEER_NKEYS
        vt = _bf(jnp.transpose(peer_v[l].reshape(-1, et, d), (0, 2, 1)))
        x = _peer_experts(h2t, _bf(peer_u[l]), vt, st.reshape((PEER_HEADS, 2) + st.shape[1:]), e2,
                          theta, coef, x1, row2(final_norm), EXPERT_TOKEN_TILE, EXPERT_ROWS_PER_STEP,
                          l == depth - 1)
        outs["p_ckv"].append(ckv[:n_p].reshape(batch, seq, KV_LORA))
        outs["p_kr"].append(kr[:n_p].reshape(batch, seq, QK_ROPE))
        outs["p_ret"].append(ret_sp)
        outs["p_hg"].append(hg_sp)
        outs["s_ckv"].append(ckv[n_p:].reshape(dec_batch, td, KV_LORA))
        outs["s_kr"].append(kr[n_p:].reshape(dec_batch, td, QK_ROPE))
        outs["s_ret"].append(ret_ss)
        outs["s_hg"].append(hg_ss)
    y_prompt = x[:n_p].reshape(batch, seq, d)
    y_sample = x[n_p:].reshape(dec_batch, td, d)
    return (y_prompt, y_sample) + tuple(jnp.stack(outs[k]) for k in
                                        ("p_ckv", "p_kr", "p_ret", "p_hg", "s_ckv", "s_kr", "s_ret", "s_hg"))
```

```python
import functools
import math

import jax
import jax.numpy as jnp
from jax import lax
from jax.experimental import pallas as pl
from jax.experimental.pallas import tpu as pltpu

D_MODEL = 1024
MLA_HEADS = 8
QK_NOPE = 64
QK_ROPE = 32
V_HEAD = 64
Q_LORA = 256
KV_LORA = 128
RET_HEADS = 4
RET_DK = 64
RET_DV = 64
HG_HEADS = 4
HG_DK = 64
HG_DV = 64
PEER_HEADS = 8
PEER_NKEYS = 128
PEER_HALF = 128
PEER_TOPK = 16
ROPE_BASE = 10000.0
EPS = 1e-6
PAGE_SIZE = 128

LANES = 128
HEAD_SLOT = 128
REC_W = 256
NEG_INF = float("-inf")
VMEM_LIMIT = 56 * 1024 * 1024

ZP_RQ, ZP_RK, ZP_RV, ZP_RG, ZP_GQ, ZP_GF, ZP_GK, ZP_GI, ZP_GG = range(9)
ZP_GROUPS = 9


def _bf(x):
    return x.astype(jnp.bfloat16)


def _dot(a, b):
    return jnp.dot(a, b, preferred_element_type=jnp.float32)


def _dot_nt(a, b):
    return lax.dot_general(a, b, (((1,), (1,)), ((), ())), preferred_element_type=jnp.float32)


def _dot_tn(a, b):
    return lax.dot_general(a, b, (((0,), (0,)), ((), ())), preferred_element_type=jnp.float32)


def _rms(x, g):
    return x * lax.rsqrt(jnp.mean(x * x, axis=-1, keepdims=True) + EPS) * g


def _silu(x):
    return x * (1.0 / (1.0 + jnp.exp(-x)))


def _split3(x):
    x1 = _bf(x)
    r1 = x - x1.astype(jnp.float32)
    x2 = _bf(r1)
    x3 = _bf(r1 - x2.astype(jnp.float32))
    return x1, x2, x3


def _proj_in_kernel(x_ref, gattn_ref, wbig_ref, qn_ref, kvn_ref, wuq_ref, wuk_ref, wuv_ref, lb_ref,
                    cq_ref, sq_ref, c64_ref, s64_ref,
                    ckv_ref, kr_ref, q_ref, k_ref, v_ref, zp_ref):
    scale = (QK_NOPE + QK_ROPE) ** -0.5 * math.log2(math.e)
    h = _bf(_rms(x_ref[...], gattn_ref[...]))
    z = _dot(h, wbig_ref[...])
    o = 0
    cq = z[:, o:o + Q_LORA]; o += Q_LORA
    ckv = z[:, o:o + KV_LORA]; o += KV_LORA
    krp = z[:, o:o + HEAD_SLOT]; o += HEAD_SLOT
    krp_rot = z[:, o:o + HEAD_SLOT]; o += HEAD_SLOT
    groups = []
    for _ in range(10):
        groups.append(z[:, o:o + REC_W]); o += REC_W
    rq, rq_rot, rk, rk_rot, rv, rg, gq, gf, gi, gg = groups

    cq_t = cq_ref[...]
    sq_t = sq_ref[...]
    cqn = _bf(_rms(cq, qn_ref[...]))
    zq = _dot(cqn, wuq_ref[...])
    hw = MLA_HEADS * HEAD_SLOT
    for hd in range(MLA_HEADS):
        a = zq[:, hd * HEAD_SLOT:(hd + 1) * HEAD_SLOT]
        b = zq[:, hw + hd * HEAD_SLOT:hw + (hd + 1) * HEAD_SLOT]
        q_ref[hd] = _bf((a * cq_t + b * sq_t) * scale)
    ckvn = _rms(ckv, kvn_ref[...])
    ckv_ref[...] = ckvn
    kr_placed = krp * cq_t + krp_rot * sq_t
    kr_ref[...] = kr_placed[:, QK_NOPE:QK_NOPE + QK_ROPE]
    ckvb = _bf(ckvn)
    kn = _dot(ckvb, wuk_ref[...])
    for hd in range(MLA_HEADS):
        k_ref[hd] = _bf(kn[:, hd * HEAD_SLOT:(hd + 1) * HEAD_SLOT] + kr_placed)
    vv = _dot(ckvb, wuv_ref[...])
    lane = lax.broadcasted_iota(jnp.int32, (vv.shape[0], HEAD_SLOT), 1)
    for hd in range(MLA_HEADS):
        v_ref[hd] = _bf(jnp.where(lane < V_HEAD, vv[:, hd * HEAD_SLOT:(hd + 1) * HEAD_SLOT], 1.0))

    c64 = c64_ref[...]
    s64 = s64_ref[...]
    zp_ref[:, ZP_RQ * REC_W:(ZP_RQ + 1) * REC_W] = rq * c64 + rq_rot * s64
    zp_ref[:, ZP_RK * REC_W:(ZP_RK + 1) * REC_W] = (rk * c64 + rk_rot * s64) * (RET_DK ** -0.5)
    zp_ref[:, ZP_RV * REC_W:(ZP_RV + 1) * REC_W] = rv
    zp_ref[:, ZP_RG * REC_W:(ZP_RG + 1) * REC_W] = rg
    lb = lb_ref[...]
    sig = 1.0 / (1.0 + jnp.exp(-gf))
    zp_ref[:, ZP_GQ * REC_W:(ZP_GQ + 1) * REC_W] = _silu(gq)
    zp_ref[:, ZP_GF * REC_W:(ZP_GF + 1) * REC_W] = jnp.log(lb + (1.0 - lb) * sig)
    zp_ref[:, ZP_GK * REC_W:(ZP_GK + 1) * REC_W] = (1.0 - lb) * (1.0 / (1.0 + jnp.exp(gf)))
    zp_ref[:, ZP_GI * REC_W:(ZP_GI + 1) * REC_W] = gi
    zp_ref[:, ZP_GG * REC_W:(ZP_GG + 1) * REC_W] = gg


def _rot_cols(w, n_heads, d):
    k = w.shape[0]
    w4 = w.reshape(k, n_heads, 2, d // 2)
    return jnp.concatenate([-w4[:, :, 1:2], w4[:, :, 0:1]], axis=2).reshape(k, n_heads * d)


def _prep_layer_weights(w_in, w_uq, w_uk, w_uv):
    sizes = (Q_LORA, KV_LORA, QK_ROPE) + (REC_W,) * 8
    offs = [0]
    for s in sizes:
        offs.append(offs[-1] + s)
    cols = [w_in[:, offs[i]:offs[i + 1]] for i in range(len(sizes))]
    cq, ckv, kr, rq, rk, rv, rg, gq, gf, gi, gg = cols
    kdim = w_in.shape[0]
    zeros = lambda n: jnp.zeros((kdim, n), w_in.dtype)
    place = lambda c: jnp.concatenate([zeros(QK_NOPE), c, zeros(HEAD_SLOT - QK_NOPE - QK_ROPE)], axis=1)
    wbig = jnp.concatenate([
        cq, ckv, place(kr), place(_rot_cols(kr, 1, QK_ROPE)),
        rq, _rot_cols(rq, RET_HEADS, RET_DK), rk, _rot_cols(rk, RET_HEADS, RET_DK),
        rv, rg, gq, gf, gi, gg], axis=1)
    wq3 = w_uq.reshape(Q_LORA, MLA_HEADS, QK_NOPE + QK_ROPE)
    nope, ropep = wq3[..., :QK_NOPE], wq3[..., QK_NOPE:]
    rope_rot = jnp.concatenate([-ropep[..., QK_ROPE // 2:], ropep[..., :QK_ROPE // 2]], axis=-1)
    pad = jnp.zeros((Q_LORA, MLA_HEADS, HEAD_SLOT - QK_NOPE - QK_ROPE), w_uq.dtype)
    plain = jnp.concatenate([nope, ropep, pad], axis=-1).reshape(Q_LORA, MLA_HEADS * HEAD_SLOT)
    rotd = jnp.concatenate([jnp.zeros_like(nope), rope_rot, pad], axis=-1).reshape(Q_LORA, MLA_HEADS * HEAD_SLOT)
    wuq = jnp.concatenate([plain, rotd], axis=1)
    wuk = jnp.concatenate([w_uk, jnp.zeros((KV_LORA, MLA_HEADS, HEAD_SLOT - QK_NOPE), w_uk.dtype)],
                          axis=-1).reshape(KV_LORA, MLA_HEADS * HEAD_SLOT)
    wuv = jnp.concatenate([w_uv, jnp.zeros((KV_LORA, MLA_HEADS, HEAD_SLOT - V_HEAD), w_uv.dtype)],
                          axis=-1).reshape(KV_LORA, MLA_HEADS * HEAD_SLOT)
    return _bf(wbig), _bf(wuq), _bf(wuk), _bf(wuv)


def _rope_tables(pos):
    posf = pos.astype(jnp.float32)[:, None]

    def cs(d):
        half = d // 2
        inv = ROPE_BASE ** (-jnp.arange(half, dtype=jnp.float32) / half)
        ang = posf * inv[None, :]
        return jnp.cos(ang), jnp.sin(ang)

    n = pos.shape[0]
    c32, s32 = cs(QK_ROPE)
    tail = jnp.zeros((n, HEAD_SLOT - QK_NOPE - QK_ROPE), jnp.float32)
    cq = jnp.concatenate([jnp.ones((n, QK_NOPE), jnp.float32), c32, c32, tail], axis=1)
    sq = jnp.concatenate([jnp.zeros((n, QK_NOPE), jnp.float32), s32, s32, tail], axis=1)
    c64, s64 = cs(RET_DK)
    c64 = jnp.tile(jnp.concatenate([c64, c64], axis=1), (1, RET_HEADS))
    s64 = jnp.tile(jnp.concatenate([s64, s64], axis=1), (1, RET_HEADS))
    return cq, sq, c64, s64


def _proj_in(x, gattn, wbig, qn, kvn, wuq, wuk, wuv, lb, tabs, tm):
    n = x.shape[0]
    cq_t, sq_t, c64, s64 = tabs
    full = lambda a: pl.BlockSpec(a.shape, lambda i: (0,) * a.ndim)
    row = lambda w: pl.BlockSpec((tm, w), lambda i: (i, 0))
    hrow = lambda nh: pl.BlockSpec((nh, tm, HEAD_SLOT), lambda i: (0, i, 0))
    out_shape = (
        jax.ShapeDtypeStruct((n, KV_LORA), jnp.float32),
        jax.ShapeDtypeStruct((n, QK_ROPE), jnp.float32),
        jax.ShapeDtypeStruct((MLA_HEADS, n, HEAD_SLOT), jnp.bfloat16),
        jax.ShapeDtypeStruct((MLA_HEADS, n, HEAD_SLOT), jnp.bfloat16),
        jax.ShapeDtypeStruct((MLA_HEADS, n, HEAD_SLOT), jnp.bfloat16),
        jax.ShapeDtypeStruct((n, ZP_GROUPS * REC_W), jnp.float32),
    )
    return pl.pallas_call(
        _proj_in_kernel,
        grid=(n // tm,),
        in_specs=[row(D_MODEL), full(gattn), full(wbig), full(qn), full(kvn), full(wuq), full(wuk), full(wuv),
                  full(lb), row(HEAD_SLOT), row(HEAD_SLOT), row(REC_W), row(REC_W)],
        out_specs=(row(KV_LORA), row(QK_ROPE), hrow(MLA_HEADS), hrow(MLA_HEADS), hrow(MLA_HEADS),
                   row(ZP_GROUPS * REC_W)),
        out_shape=out_shape,
        compiler_params=pltpu.CompilerParams(dimension_semantics=("parallel",), vmem_limit_bytes=VMEM_LIMIT),
        name="proj_in",
    )(x, gattn, wbig, qn, kvn, wuq, wuk, wuv, lb, cq_t, sq_t, c64, s64)


def _online_step(s, vb, m, acc):
    m_new = jnp.maximum(m, jnp.max(s, axis=-1, keepdims=True))
    acc = jnp.exp2(m - m_new) * acc + _dot(_bf(jnp.exp2(s - m_new)), vb)
    return m_new, acc


def _attn_prompt_kernel(q_ref, k_ref, v_ref, o_ref, *, blk):
    qi = pl.program_id(2)
    qs = (q_ref[0], q_ref[1])

    def step(j, carry, masked):
        start = pl.multiple_of(j * blk, blk)
        out = []
        for hh in range(2):
            s = _dot_nt(qs[hh], k_ref[hh, pl.ds(start, blk), :])
            if masked:
                row = lax.broadcasted_iota(jnp.int32, (blk, blk), 0)
                col = lax.broadcasted_iota(jnp.int32, (blk, blk), 1)
                s = jnp.where(col <= row, s, NEG_INF)
            out.append(_online_step(s, v_ref[hh, pl.ds(start, blk), :], *carry[hh]))
        return tuple(out)

    init = (jnp.full((blk, 1), NEG_INF, jnp.float32), jnp.zeros((blk, LANES), jnp.float32))
    pairs = qi // 2
    carry = lax.fori_loop(0, pairs, lambda j, c: step(2 * j + 1, step(2 * j, c, False), False), (init, init))
    carry = lax.fori_loop(2 * pairs, qi, lambda j, c: step(j, c, False), carry)
    carry = step(qi, carry, True)
    norm = [acc / pltpu.roll(acc, V_HEAD, axis=1) for _, acc in carry]
    lane = lax.broadcasted_iota(jnp.int32, (blk, LANES), 1)
    o_ref[...] = _bf(jnp.where(lane < V_HEAD, norm[0], pltpu.roll(norm[1], V_HEAD, axis=1)))


def _attn_prompt(q, k, v, batch, seq, blk):
    nq = seq // blk
    return pl.pallas_call(
        functools.partial(_attn_prompt_kernel, blk=blk),
        grid=(batch, MLA_HEADS // 2, nq),
        in_specs=[pl.BlockSpec((2, blk, HEAD_SLOT), lambda b, p, i: (p, b * nq + i, 0)),
                  pl.BlockSpec((2, seq, HEAD_SLOT), lambda b, p, i: (p, b, 0)),
                  pl.BlockSpec((2, seq, LANES), lambda b, p, i: (p, b, 0))],
        out_specs=pl.BlockSpec((blk, LANES), lambda b, p, i: (b * nq + i, p)),
        out_shape=jax.ShapeDtypeStruct((batch * seq, MLA_HEADS * V_HEAD), jnp.bfloat16),
        compiler_params=pltpu.CompilerParams(dimension_semantics=("parallel", "parallel", "arbitrary"),
                                             vmem_limit_bytes=VMEM_LIMIT),
        name="attn_prompt",
    )(q, k, v)


def _attn_sample_kernel(pt_ref, q_ref, wuk_ref, wuvp_ref, ckv_ref, kr_ref, *rest, pages_per_step, td):
    kv_refs = rest[:pages_per_step]
    krt_refs = rest[pages_per_step:2 * pages_per_step]
    o_ref = rest[2 * pages_per_step]
    qlat_ref, qrope_ref, m_ref, l_ref, acc_ref, kall_ref, krt_ref = rest[2 * pages_per_step + 1:]
    j = pl.program_id(1)
    rows = MLA_HEADS * td

    @pl.when(j == 0)
    def _():
        for hd in range(MLA_HEADS):
            qh = q_ref[hd].astype(jnp.float32)
            qlat_ref[hd * td:(hd + 1) * td, :] = _dot_nt(_bf(qh[:, :QK_NOPE]), wuk_ref[hd])
            qrope_ref[hd * td:(hd + 1) * td, :] = qh[:, QK_NOPE:QK_NOPE + QK_ROPE]
        m_ref[...] = jnp.full((rows, 1), NEG_INF, jnp.float32)
        l_ref[...] = jnp.zeros((rows, 1), jnp.float32)
        acc_ref[...] = jnp.zeros((rows, KV_LORA), jnp.float32)

    for r in range(pages_per_step):
        kall_ref[r * PAGE_SIZE:(r + 1) * PAGE_SIZE, :] = _bf(kv_refs[r][...])
        krt_ref[:, r * PAGE_SIZE:(r + 1) * PAGE_SIZE] = _bf(krt_refs[r][...])
    qlat = _bf(qlat_ref[...])
    qrope = _bf(qrope_ref[...])
    kall = kall_ref[...]
    s = _dot_nt(qlat, kall) + _dot(qrope, krt_ref[...])
    m = m_ref[...]
    m_new = jnp.maximum(m, jnp.max(s, axis=-1, keepdims=True))
    alpha = jnp.exp2(m - m_new)
    p = jnp.exp2(s - m_new)
    l = alpha * l_ref[...] + jnp.sum(p, axis=-1, keepdims=True)
    acc = alpha * acc_ref[...] + _dot(_bf(p), kall)
    m_ref[...], l_ref[...], acc_ref[...] = m_new, l, acc

    @pl.when(j == pl.num_programs(1) - 1)
    def _():
        pad = jnp.zeros((PAGE_SIZE - td, KV_LORA), jnp.float32)
        cnew = _bf(jnp.concatenate([ckv_ref[...], pad], axis=0))
        krnew = _bf(jnp.concatenate([kr_ref[...], pad[:, :QK_ROPE]], axis=0))
        sn = _dot_nt(qlat, cnew) + _dot_nt(qrope, krnew)
        row = lax.broadcasted_iota(jnp.int32, (rows, PAGE_SIZE), 0)
        col = lax.broadcasted_iota(jnp.int32, (rows, PAGE_SIZE), 1)
        sn = jnp.where(col <= row % td, sn, NEG_INF)
        m2 = jnp.maximum(m_new, jnp.max(sn, axis=-1, keepdims=True))
        a2 = jnp.exp2(m_new - m2)
        pn = jnp.exp2(sn - m2)
        l2 = a2 * l + jnp.sum(pn, axis=-1, keepdims=True)
        acc2 = a2 * acc + _dot(_bf(pn), cnew)
        olat = _bf(acc2 / l2)
        out = jnp.zeros((td, MLA_HEADS * V_HEAD), jnp.float32)
        for hd in range(MLA_HEADS):
            out = out + _dot(olat[hd * td:(hd + 1) * td, :], wuvp_ref[hd])
        o_ref[...] = out


def _attn_sample(page_table, q, wuk_h, wuv_placed, ckv, kr, cache_kv, cache_krt, layer, row0, dec_batch, td,
                 pages_per_step):
    n_pages = page_table.shape[1]
    steps = n_pages // pages_per_step
    blk0 = row0 // td
    rows = MLA_HEADS * td
    keys = pages_per_step * PAGE_SIZE

    def page_spec(shape, r):
        return pl.BlockSpec((None, None) + shape,
                            lambda b, j, pt, r=r: (layer, pt[b * n_pages + j * pages_per_step + r], 0, 0))

    in_specs = [pl.BlockSpec((MLA_HEADS, td, HEAD_SLOT), lambda b, j, pt: (0, blk0 + b, 0)),
                pl.BlockSpec(wuk_h.shape, lambda b, j, pt: (0, 0, 0)),
                pl.BlockSpec(wuv_placed.shape, lambda b, j, pt: (0, 0, 0)),
                pl.BlockSpec((td, KV_LORA), lambda b, j, pt: (blk0 + b, 0)),
                pl.BlockSpec((td, QK_ROPE), lambda b, j, pt: (blk0 + b, 0))]
    in_specs += [page_spec((PAGE_SIZE, KV_LORA), r) for r in range(pages_per_step)]
    in_specs += [page_spec((QK_ROPE, PAGE_SIZE), r) for r in range(pages_per_step)]
    grid_spec = pltpu.PrefetchScalarGridSpec(
        num_scalar_prefetch=1,
        grid=(dec_batch, steps),
        in_specs=in_specs,
        out_specs=pl.BlockSpec((td, MLA_HEADS * V_HEAD), lambda b, j, pt: (b, 0)),
        scratch_shapes=[pltpu.VMEM((rows, KV_LORA), jnp.float32), pltpu.VMEM((rows, QK_ROPE), jnp.float32),
                        pltpu.VMEM((rows, 1), jnp.float32), pltpu.VMEM((rows, 1), jnp.float32),
                        pltpu.VMEM((rows, KV_LORA), jnp.float32),
                        pltpu.VMEM((keys, KV_LORA), jnp.bfloat16), pltpu.VMEM((QK_ROPE, keys), jnp.bfloat16)])
    return pl.pallas_call(
        functools.partial(_attn_sample_kernel, pages_per_step=pages_per_step, td=td),
        grid_spec=grid_spec,
        out_shape=jax.ShapeDtypeStruct((dec_batch * td, MLA_HEADS * V_HEAD), jnp.float32),
        compiler_params=pltpu.CompilerParams(dimension_semantics=("parallel", "arbitrary"),
                                             vmem_limit_bytes=VMEM_LIMIT),
        name="attn_sample",
    )(page_table.reshape(-1), q, wuk_h, wuv_placed, ckv, kr, *([cache_kv] * pages_per_step),
      *([cache_krt] * pages_per_step))


def _place_wuv(w_uv):
    eye = jnp.eye(MLA_HEADS, dtype=w_uv.dtype)
    placed = jnp.einsum('chv,hg->hcgv', w_uv, eye)
    return _bf(placed.reshape(MLA_HEADS, KV_LORA, MLA_HEADS * V_HEAD))


def _ret_gammas():
    return [1.0 - 2.0 ** (-5.0 - h) for h in range(RET_HEADS)]


def _ret_kernel(*refs, nb, parts, decay_len):
    q_refs, k_refs, v_refs, g_refs = (refs[g * parts:(g + 1) * parts] for g in range(4))
    s0_ref, dmask_ref, qdec_ref, kdec_ref, gn_ref, o_ref, sfin_ref, s_ref = refs[4 * parts:]
    c = pl.program_id(1)

    @pl.when(c == 0)
    def _():
        s_ref[...] = s0_ref[...]

    qdec = qdec_ref[...]
    kdec = kdec_ref[...]
    gn = gn_ref[...]
    for sq in range(parts * nb):
        p, bi = divmod(sq, nb)
        q = q_refs[p][bi]
        k = k_refs[p][bi]
        v = v_refs[p][bi]
        kd = k * kdec
        gate = _silu(g_refs[p][bi])
        for h in range(RET_HEADS):
            sl = slice(h * RET_DK, (h + 1) * RET_DK)
            qh, kh, vh = _bf(q[:, sl]), _bf(k[:, sl]), _bf(v[:, sl])
            a = _dot_nt(qh, kh) * dmask_ref[h]
            s = s_ref[sq, h]
            o = _dot(_bf(a), vh) + _dot(qh, _bf(s)) * qdec[:, sl]
            s_ref[sq, h] = s * (_ret_gammas()[h] ** decay_len) + _dot_tn(_bf(kd[:, sl]), vh)
            oc = o - jnp.mean(o, axis=-1, keepdims=True)
            y = oc * lax.rsqrt(jnp.mean(oc * oc, axis=-1, keepdims=True) + EPS)
            o_ref[sq, :, sl] = _bf(y * gn[:, sl] * gate[:, sl])

    @pl.when(c == pl.num_programs(1) - 1)
    def _():
        sfin_ref[...] = s_ref[...]


def _ret_consts(chunk, decay_len):
    lg = jnp.log(jnp.asarray(_ret_gammas(), jnp.float32))
    idx = jnp.arange(chunk, dtype=jnp.float32)
    diff = idx[:, None] - idx[None, :]
    dmask = jnp.where(diff >= 0, jnp.exp(lg[:, None, None] * jnp.maximum(diff, 0.0)), 0.0)
    qdec = jnp.exp(lg[None, :] * (idx[:, None] + 1.0))
    kdec = jnp.exp(lg[None, :] * (decay_len - 1.0 - idx[:, None]))
    rep = lambda a: jnp.repeat(a, RET_DK, axis=1)
    return dmask, rep(qdec), rep(kdec)


def _rec_specs(nb, parts, chunk, nch, groups):
    return [pl.BlockSpec((nb, chunk, REC_W), lambda b, c, g=g, p=p: ((b * parts + p) * nch + c, 0, g))
            for g in groups for p in range(parts)]


def _rec_call(kernel_fn, name, groups, zp3, s0, extra, nb, parts, nch):
    chunk = zp3.shape[1]
    seqs = s0.shape[0]
    per_step = parts * nb
    full = lambda a: pl.BlockSpec(a.shape, lambda b, c: (0,) * a.ndim)
    st = pl.BlockSpec((per_step,) + s0.shape[1:], lambda b, c: (b, 0, 0, 0))
    return pl.pallas_call(
        kernel_fn,
        grid=(seqs // per_step, nch),
        in_specs=_rec_specs(nb, parts, chunk, nch, groups) + [st] + [full(a) for a in extra],
        out_specs=(pl.BlockSpec((per_step, None, chunk, REC_W), lambda b, c: (b, c, 0, 0)), st),
        out_shape=(jax.ShapeDtypeStruct((seqs, nch, chunk, REC_W), jnp.bfloat16),
                   jax.ShapeDtypeStruct(s0.shape, jnp.float32)),
        scratch_shapes=[pltpu.VMEM((per_step,) + s0.shape[1:], jnp.float32)],
        compiler_params=pltpu.CompilerParams(dimension_semantics=("parallel", "arbitrary"),
                                             vmem_limit_bytes=VMEM_LIMIT),
        name=name,
    )(*([zp3] * (len(groups) * parts)), s0, *extra)


def _retention(zp3, s0, gn, nb, parts, nch, decay_len):
    consts = _ret_consts(zp3.shape[1], decay_len)
    return _rec_call(functools.partial(_ret_kernel, nb=nb, parts=parts, decay_len=decay_len), "retention",
                     (ZP_RQ, ZP_RK, ZP_RV, ZP_RG), zp3, s0, consts + (gn,), nb, parts, nch)


def _hgrn_sum_matrix(chunk):
    import numpy as np
    t = np.arange(chunk)[:, None]
    r = np.arange(chunk)[None, :]
    blocks = [r <= t, r > t]
    m = 2
    while m <= chunk:
        half = m // 2
        mid = (t // m) * m + half
        upper = (t % m) >= half
        blocks.append(upper & (r >= mid) & (r <= t))
        blocks.append(~upper & (r > t) & (r <= mid - 1))
        m *= 2
    return jnp.asarray(np.concatenate(blocks, axis=0).astype(np.float32), jnp.bfloat16)


def _hgrn_kernel(*refs, nb, parts, chunk):
    q_refs, f_refs, k_refs, v_refs, g_refs = (refs[g * parts:(g + 1) * parts] for g in range(5))
    s0_ref, msum_ref, gn_ref, o_ref, sfin_ref, st_ref = refs[5 * parts:]
    c = pl.program_id(1)
    nlev = chunk.bit_length() - 1

    @pl.when(c == 0)
    def _():
        for bi in range(parts * nb):
            for h in range(HG_HEADS):
                st_ref[bi, h] = s0_ref[bi, h].T

    gn = gn_ref[...]
    rowi = lax.broadcasted_iota(jnp.int32, (chunk, REC_W), 0)
    r2 = lax.broadcasted_iota(jnp.int32, (chunk, chunk), 0)
    c2 = lax.broadcasted_iota(jnp.int32, (chunk, chunk), 1)
    for sq in range(parts * nb):
        p, bi = divmod(sq, nb)
        q = q_refs[p][bi]
        k = k_refs[p][bi]
        v = v_refs[p][bi]
        gate = _silu(g_refs[p][bi])
        f1, f2, f3 = _split3(f_refs[p][bi])
        sums = _dot(msum_ref[...], jnp.concatenate([f1, f2, f3], axis=1))
        sums = sums[:, :REC_W] + sums[:, REC_W:2 * REC_W] + sums[:, 2 * REC_W:]
        part = lambda i: sums[i * chunk:(i + 1) * chunk]
        b = part(0)
        qb = _bf(q * jnp.exp(b))
        ks = _bf(k * jnp.exp(part(1)))
        e_last = jnp.exp(b[chunk - 1:chunk, :])
        vb = _bf(v)
        qs, kls = [_bf(q)], [_bf(k)]
        for lev in range(1, nlev + 1):
            upper = (rowi & (1 << (lev - 1))) != 0
            qs.append(_bf(jnp.where(upper, q * jnp.exp(part(2 * lev)), 0.0)))
            kls.append(_bf(jnp.where(upper, 0.0, k * jnp.exp(part(2 * lev + 1)))))
        for h in range(HG_HEADS):
            sl = slice(h * HG_DK, (h + 1) * HG_DK)
            a = jnp.where(r2 == c2, _dot_nt(qs[0][:, sl], kls[0][:, sl]), 0.0)
            for lev in range(1, nlev + 1):
                same = (r2 >> lev) == (c2 >> lev)
                a = a + jnp.where(same, _dot_nt(qs[lev][:, sl], kls[lev][:, sl]), 0.0)
            st = st_ref[sq, h]
            o = _dot(_bf(a), vb[:, sl]) + _dot_nt(qb[:, sl], _bf(st))
            st_ref[sq, h] = st * e_last[:, sl] + _dot_tn(vb[:, sl], ks[:, sl])
            y = o * lax.rsqrt(jnp.mean(o * o, axis=-1, keepdims=True) + EPS)
            o_ref[sq, :, sl] = _bf(y * gn[:, sl] * gate[:, sl])

    @pl.when(c == pl.num_programs(1) - 1)
    def _():
        for bi in range(parts * nb):
            for h in range(HG_HEADS):
                sfin_ref[bi, h] = st_ref[bi, h].T


def _hgrn(zp3, s0, gn, nb, parts, nch):
    chunk = zp3.shape[1]
    return _rec_call(functools.partial(_hgrn_kernel, nb=nb, parts=parts, chunk=chunk), "hgrn2",
                     (ZP_GQ, ZP_GF, ZP_GK, ZP_GI, ZP_GG), zp3, s0, (_hgrn_sum_matrix(chunk), gn), nb, parts, nch)


def _dot3(ah, al, bh, bl, dot):
    return dot(ah, bh) + dot(al, bh) + dot(ah, bl)


def _split2(x):
    hi = _bf(x)
    return hi, _bf(x - hi.astype(jnp.float32))


def _out_proj_kernel(x_ref, omla_ref, oret_ref, ohg_ref, wout_ref, gffn_ref, wqh_ref, wql_ref, kh_ref, kl_ref,
                     x1_ref, h2t_ref, st_ref):
    mix = jnp.concatenate([omla_ref[...], oret_ref[...], ohg_ref[...]], axis=1)
    x1 = x_ref[...] + _dot(mix, wout_ref[...])
    x1_ref[...] = x1
    h2 = _rms(x1, gffn_ref[...])
    h2t_ref[...] = _bf(h2.T)
    hh, hl = _split2(h2)
    q = _dot3(hh, hl, wqh_ref[...], wql_ref[...], _dot)
    for p in range(2 * PEER_HEADS):
        qh, ql = _split2(q[:, p * PEER_HALF:(p + 1) * PEER_HALF])
        sp = _dot3(kh_ref[p], kl_ref[p], qh, ql, _dot_nt)
        for c in range(sp.shape[1] // LANES):
            st_ref[p, c] = sp[:, c * LANES:(c + 1) * LANES]


def _out_proj(x, omla, oret, ohg, wout, gffn, wqh, wql, kh, kl, tm):
    n = x.shape[0]
    full = lambda a: pl.BlockSpec(a.shape, lambda i: (0,) * a.ndim)
    row = lambda w: pl.BlockSpec((tm, w), lambda i: (i, 0))
    return pl.pallas_call(
        _out_proj_kernel,
        grid=(n // tm,),
        in_specs=[row(D_MODEL), row(MLA_HEADS * V_HEAD), row(REC_W), row(REC_W), full(wout), full(gffn), full(wqh),
                  full(wql), full(kh), full(kl)],
        out_specs=(row(D_MODEL), pl.BlockSpec((D_MODEL, tm), lambda i: (0, i)),
                   pl.BlockSpec((2 * PEER_HEADS, tm // LANES, PEER_NKEYS, LANES), lambda i: (0, i, 0, 0))),
        out_shape=(jax.ShapeDtypeStruct((n, D_MODEL), jnp.float32),
                   jax.ShapeDtypeStruct((D_MODEL, n), jnp.bfloat16),
                   jax.ShapeDtypeStruct((2 * PEER_HEADS, n // LANES, PEER_NKEYS, LANES), jnp.float32)),
        compiler_params=pltpu.CompilerParams(dimension_semantics=("parallel",), vmem_limit_bytes=VMEM_LIMIT),
        name="out_proj_peer_scores",
    )(x, omla, oret, ohg, wout, gffn, wqh, wql, kh, kl)


def _extract_top(x, count, on_value):
    rowf = lax.broadcasted_iota(jnp.int32, x.shape, 1).astype(jnp.float32)
    for r in range(count):
        m = jnp.max(x, axis=1, keepdims=True)
        first = jnp.min(jnp.where(x == m, rowf, float(x.shape[1])), axis=1, keepdims=True)
        hit = rowf == first
        on_value(r, m, hit)
        x = jnp.where(hit, NEG_INF, x)
    return x


SUB = 8


G_S2, G_E2, G_THETA, G_COEF = range(4)


def _peer_select_kernel(s_ref, gates_ref, v1_ref, v2_ref, cand_ref):
    s1 = s_ref[:, 0]
    s2 = s_ref[:, 1]

    def keep(ref):
        def on_value(r, m, hit):
            ref[:, r:r + 1, :] = m
        return on_value

    _extract_top(s1, PEER_TOPK + 1, keep(v1_ref))
    _extract_top(s2, PEER_TOPK + 1, keep(v2_ref))
    for r1 in range(SUB):
        cand_ref[:, r1 * SUB:(r1 + 1) * SUB, :] = v1_ref[:, r1:r1 + 1, :] + v2_ref[:, 0:SUB, :]
    cand_ref[:, SUB * SUB:SUB * SUB + SUB, :] = v1_ref[:, 0:1, :] + v2_ref[:, SUB:2 * SUB, :]
    cand_ref[:, SUB * SUB + SUB:SUB * SUB + 2 * SUB, :] = v1_ref[:, SUB:2 * SUB, :] + v2_ref[:, 0:1, :]
    best = v1_ref[:, 0:1, :] + v2_ref[:, 0:1, :]
    picked = []
    rest = _extract_top(cand_ref[...], PEER_TOPK, lambda r, m, hit: picked.append(m))
    z = jnp.zeros_like(best)
    for m in picked:
        z = z + jnp.exp(m - best)
    nxt = jnp.max(rest, axis=1, keepdims=True)
    nxt = jnp.maximum(nxt, v1_ref[:, PEER_TOPK:PEER_TOPK + 1, :] + v2_ref[:, 0:1, :])
    nxt = jnp.maximum(nxt, v1_ref[:, 0:1, :] + v2_ref[:, PEER_TOPK:PEER_TOPK + 1, :])
    tau = 0.5 * (picked[-1] + nxt)
    gates_ref[:, G_S2] = s2
    gates_ref[:, G_E2] = jnp.exp(s2 - v2_ref[:, 0:1, :])
    gates_ref[:, G_THETA] = tau - s1
    gates_ref[:, G_COEF] = jnp.exp(s1 - v1_ref[:, 0:1, :]) * (1.0 / z)


def _peer_select(st):
    nt = st.shape[1]
    s5 = st.reshape(PEER_HEADS, 2, nt, PEER_NKEYS, LANES)
    vrows = SUB * ((PEER_TOPK + 1 + SUB - 1) // SUB)
    return pl.pallas_call(
        _peer_select_kernel,
        grid=(nt,),
        in_specs=[pl.BlockSpec((PEER_HEADS, 2, None, PEER_NKEYS, LANES), lambda t: (0, 0, t, 0, 0))],
        out_specs=pl.BlockSpec((PEER_HEADS, None, 4, PEER_NKEYS, LANES), lambda t: (0, t, 0, 0, 0)),
        out_shape=jax.ShapeDtypeStruct((PEER_HEADS, nt, 4, PEER_NKEYS, LANES), jnp.float32),
        scratch_shapes=[pltpu.VMEM((PEER_HEADS, vrows, LANES), jnp.float32),
                        pltpu.VMEM((PEER_HEADS, vrows, LANES), jnp.float32),
                        pltpu.VMEM((PEER_HEADS, SUB * SUB + 2 * SUB, LANES), jnp.float32)],
        compiler_params=pltpu.CompilerParams(dimension_semantics=("parallel",), vmem_limit_bytes=VMEM_LIMIT),
        name="peer_select",
    )(s5)


def _peer_expert_kernel(h2t_ref, u_ref, vt_ref, gates_ref, x1_ref, gfin_ref, o_ref, acc_ref, p_ref, at_ref, *,
                        ni, final_norm):
    step = pl.program_id(1)

    @pl.when(step == 0)
    def _():
        acc_ref[...] = jnp.zeros_like(acc_ref)

    tm = h2t_ref.shape[1]
    strips = tm // LANES
    at_ref[...] = _dot(u_ref[...], h2t_ref[...])

    def across(h, plane, rows):
        return jnp.concatenate([gates_ref[h, c, plane, rows, :] for c in range(strips)], axis=1)

    every = slice(None)
    for ib in range(ni):
        row_i = pl.ds(step * ni + ib, 1)
        w = None
        for h in range(PEER_HEADS):
            sel = across(h, G_S2, every) >= across(h, G_THETA, row_i)
            term = jnp.where(sel, across(h, G_E2, every) * across(h, G_COEF, row_i), 0.0)
            w = term if w is None else w + term
        rows = slice(ib * PEER_NKEYS, (ib + 1) * PEER_NKEYS)
        a = at_ref[rows, :]
        gelu = 0.5 * a * (1.0 + lax.erf(a * (2.0 ** -0.5)))
        p_ref[rows, :] = _bf(gelu * w)
    acc_ref[...] += _dot(vt_ref[...], p_ref[...])

    @pl.when(step == pl.num_programs(1) - 1)
    def _():
        x2 = x1_ref[...] + acc_ref[...].T
        o_ref[...] = _rms(x2, gfin_ref[...]) if final_norm else x2


def _peer_experts(h2t, u, vt, gates, x1, gfin, tm, ni, final_norm):
    n = x1.shape[0]
    et = ni * PEER_NKEYS
    blocks = PEER_NKEYS // ni
    return pl.pallas_call(
        functools.partial(_peer_expert_kernel, ni=ni, final_norm=final_norm),
        grid=(n // tm, blocks),
        in_specs=[pl.BlockSpec((D_MODEL, tm), lambda t, i: (0, t)),
                  pl.BlockSpec((et, D_MODEL), lambda t, i: (i, 0)),
                  pl.BlockSpec((None, D_MODEL, et), lambda t, i: (i, 0, 0)),
                  pl.BlockSpec((PEER_HEADS, tm // LANES, 4, PEER_NKEYS, LANES), lambda t, i: (0, t, 0, 0, 0)),
                  pl.BlockSpec((tm, D_MODEL), lambda t, i: (t, 0)),
                  pl.BlockSpec(gfin.shape, lambda t, i: (0, 0))],
        out_specs=pl.BlockSpec((tm, D_MODEL), lambda t, i: (t, 0)),
        out_shape=jax.ShapeDtypeStruct((n, D_MODEL), jnp.float32),
        scratch_shapes=[pltpu.VMEM((D_MODEL, tm), jnp.float32), pltpu.VMEM((et, tm), jnp.bfloat16),
                        pltpu.VMEM((et, tm), jnp.float32)],
        compiler_params=pltpu.CompilerParams(dimension_semantics=("parallel", "arbitrary"),
                                             vmem_limit_bytes=VMEM_LIMIT),
        name="peer_experts",
    )(h2t, u, vt, gates, x1, gfin)


TOKEN_TILE = 256
ATTN_BLOCK = 512
REC_CHUNK = 128
SAMPLE_CHUNK = 16
SAMPLE_SEQS_PER_STEP = 8
PAGES_PER_STEP = 64
EXPERT_TOKEN_TILE = 512
EXPERT_ROWS_PER_STEP = 8


def kernel(x_prompt, x_sample, cache_kv_latent, cache_k_rope, state_retention, state_hgrn, page_table,
           w_in, q_norm, w_uq, kv_norm, w_uk, w_uv, ret_norm, hg_norm, hg_lower_bounds, w_out,
           attn_norm, ffn_norm, final_norm, peer_wq, peer_keys, peer_u, peer_v):
    batch, seq, d = x_prompt.shape
    dec_batch, td, _ = x_sample.shape
    depth = w_in.shape[0]
    n_p, n_s = batch * seq, dec_batch * td
    past_len = page_table.shape[1] * PAGE_SIZE
    f32 = jnp.float32

    x = jnp.concatenate([x_prompt.reshape(n_p, d), x_sample.reshape(n_s, d)], axis=0)
    pos = jnp.concatenate([jnp.tile(jnp.arange(seq, dtype=jnp.int32), batch),
                           jnp.tile(past_len + jnp.arange(td, dtype=jnp.int32), dec_batch)])
    tabs = _rope_tables(pos)
    cache_krt = jnp.swapaxes(cache_k_rope, 2, 3)
    pages_per_step = math.gcd(PAGES_PER_STEP, page_table.shape[1])
    lb_soft = jax.nn.softmax(hg_lower_bounds.astype(f32), axis=0)
    lbs = jnp.cumsum(lb_soft, axis=0) - lb_soft[0]
    zeros_state = jnp.zeros((batch, RET_HEADS, RET_DK, RET_DV), f32)
    nch = seq // REC_CHUNK
    row2 = lambda a: a.reshape(1, -1)

    outs = {k: [] for k in ("p_ckv", "p_kr", "p_ret", "p_hg", "s_ckv", "s_kr", "s_ret", "s_hg")}
    for l in range(depth):
        wbig, wuq, wuk, wuv = _prep_layer_weights(w_in[l], w_uq[l], w_uk[l], w_uv[l])
        ckv, kr, q, k, v, zp = _proj_in(x, row2(attn_norm[l]), wbig, row2(q_norm[l]), row2(kv_norm[l]), wuq, wuk,
                                        wuv, row2(lbs[l]), tabs, TOKEN_TILE)
        o_p = _attn_prompt(q, k, v, batch, seq, ATTN_BLOCK)
        o_s = _attn_sample(page_table, q, _bf(jnp.transpose(w_uk[l], (1, 0, 2))), _place_wuv(w_uv[l]), ckv, kr,
                           cache_kv_latent, cache_krt, l, n_p, dec_batch, td, pages_per_step)
        o_mla = jnp.concatenate([o_p, _bf(o_s)], axis=0)
        zp_p = zp.reshape(-1, REC_CHUNK, zp.shape[-1])
        zp_s = jnp.pad(zp[n_p:].reshape(dec_batch, td, -1), ((0, 0), (0, SAMPLE_CHUNK - td), (0, 0)))
        gn_ret, gn_hg = row2(ret_norm[l]), row2(hg_norm[l])
        ret_p, ret_sp = _retention(zp_p, zeros_state, gn_ret, 1, batch, nch, REC_CHUNK)
        ret_s, ret_ss = _retention(zp_s, state_retention[l], gn_ret, SAMPLE_SEQS_PER_STEP, 1, 1, td)
        hg_p, hg_sp = _hgrn(zp_p, zeros_state, gn_hg, 1, batch, nch)
        hg_s, hg_ss = _hgrn(zp_s, state_hgrn[l], gn_hg, SAMPLE_SEQS_PER_STEP, 1, 1)
        join = lambda p, s: jnp.concatenate([p.reshape(n_p, REC_W), s[:, 0, :td].reshape(n_s, REC_W)], axis=0)
        o_ret, o_hg = join(ret_p, ret_s), join(hg_p, hg_s)
        wqh, wql = _split2(peer_wq[l])
        kh, kl = _split2(peer_keys[l].reshape(2 * PEER_HEADS, PEER_NKEYS, PEER_HALF))
        x1, h2t, st = _out_proj(x, o_mla, o_ret, o_hg, _bf(w_out[l]), row2(ffn_norm[l]), wqh, wql, kh, kl,
                                TOKEN_TILE)
        gates = _peer_select(st)
        et = EXPERT_ROWS_PER_STEP * PEER_NKEYS
        vt = _bf(jnp.transpose(peer_v[l].reshape(-1, et, d), (0, 2, 1)))
        x = _peer_experts(h2t, _bf(peer_u[l]), vt, gates, x1, row2(final_norm), EXPERT_TOKEN_TILE,
                          EXPERT_ROWS_PER_STEP, l == depth - 1)
        outs["p_ckv"].append(ckv[:n_p].reshape(batch, seq, KV_LORA))
        outs["p_kr"].append(kr[:n_p].reshape(batch, seq, QK_ROPE))
        outs["p_ret"].append(ret_sp)
        outs["p_hg"].append(hg_sp)
        outs["s_ckv"].append(ckv[n_p:].reshape(dec_batch, td, KV_LORA))
        outs["s_kr"].append(kr[n_p:].reshape(dec_batch, td, QK_ROPE))
        outs["s_ret"].append(ret_ss)
        outs["s_hg"].append(hg_ss)
    y_prompt = x[:n_p].reshape(batch, seq, d)
    y_sample = x[n_p:].reshape(dec_batch, td, d)
    return (y_prompt, y_sample) + tuple(jnp.stack(outs[k]) for k in
                                        ("p_ckv", "p_kr", "p_ret", "p_hg", "s_ckv", "s_kr", "s_ret", "s_hg"))
```

```python
import functools
import math

import jax
import jax.numpy as jnp
from jax import lax
from jax.experimental import pallas as pl
from jax.experimental.pallas import tpu as pltpu

D_MODEL = 1024
MLA_HEADS = 8
QK_NOPE = 64
QK_ROPE = 32
V_HEAD = 64
Q_LORA = 256
KV_LORA = 128
RET_HEADS = 4
RET_DK = 64
RET_DV = 64
HG_HEADS = 4
HG_DK = 64
HG_DV = 64
PEER_HEADS = 8
PEER_NKEYS = 128
PEER_HALF = 128
PEER_TOPK = 16
ROPE_BASE = 10000.0
EPS = 1e-6
PAGE_SIZE = 128

LANES = 128
HEAD_SLOT = 128
REC_W = 256
NEG_INF = float("-inf")
VMEM_LIMIT = 56 * 1024 * 1024

ZP_RQ, ZP_RK, ZP_RV, ZP_RG, ZP_GQ, ZP_GF, ZP_GK, ZP_GI, ZP_GG = range(9)
ZP_GROUPS = 9


def _bf(x):
    return x.astype(jnp.bfloat16)


def _dot(a, b):
    return jnp.dot(a, b, preferred_element_type=jnp.float32)


def _dot_nt(a, b):
    return lax.dot_general(a, b, (((1,), (1,)), ((), ())), preferred_element_type=jnp.float32)


def _dot_tn(a, b):
    return lax.dot_general(a, b, (((0,), (0,)), ((), ())), preferred_element_type=jnp.float32)


def _rms(x, g):
    return x * lax.rsqrt(jnp.mean(x * x, axis=-1, keepdims=True) + EPS) * g


def _silu(x):
    return x * (1.0 / (1.0 + jnp.exp(-x)))


def _split3(x):
    x1 = _bf(x)
    r1 = x - x1.astype(jnp.float32)
    x2 = _bf(r1)
    x3 = _bf(r1 - x2.astype(jnp.float32))
    return x1, x2, x3


def _proj_in_kernel(x_ref, gattn_ref, wbig_ref, qn_ref, kvn_ref, wuq_ref, wuk_ref, wuv_ref, lb_ref,
                    cq_ref, sq_ref, c64_ref, s64_ref,
                    ckv_ref, kr_ref, q_ref, k_ref, v_ref, zp_ref):
    scale = (QK_NOPE + QK_ROPE) ** -0.5 * math.log2(math.e)
    h = _bf(_rms(x_ref[...], gattn_ref[...]))
    z = _dot(h, wbig_ref[...])
    o = 0
    cq = z[:, o:o + Q_LORA]; o += Q_LORA
    ckv = z[:, o:o + KV_LORA]; o += KV_LORA
    krp = z[:, o:o + HEAD_SLOT]; o += HEAD_SLOT
    krp_rot = z[:, o:o + HEAD_SLOT]; o += HEAD_SLOT
    groups = []
    for _ in range(10):
        groups.append(z[:, o:o + REC_W]); o += REC_W
    rq, rq_rot, rk, rk_rot, rv, rg, gq, gf, gi, gg = groups

    cq_t = cq_ref[...]
    sq_t = sq_ref[...]
    cqn = _bf(_rms(cq, qn_ref[...]))
    zq = _dot(cqn, wuq_ref[...])
    hw = MLA_HEADS * HEAD_SLOT
    for hd in range(MLA_HEADS):
        a = zq[:, hd * HEAD_SLOT:(hd + 1) * HEAD_SLOT]
        b = zq[:, hw + hd * HEAD_SLOT:hw + (hd + 1) * HEAD_SLOT]
        q_ref[hd] = _bf((a * cq_t + b * sq_t) * scale)
    ckvn = _rms(ckv, kvn_ref[...])
    ckv_ref[...] = ckvn
    kr_placed = krp * cq_t + krp_rot * sq_t
    kr_ref[...] = kr_placed[:, QK_NOPE:QK_NOPE + QK_ROPE]
    ckvb = _bf(ckvn)
    kn = _dot(ckvb, wuk_ref[...])
    for hd in range(MLA_HEADS):
        k_ref[hd] = _bf(kn[:, hd * HEAD_SLOT:(hd + 1) * HEAD_SLOT] + kr_placed)
    vv = _dot(ckvb, wuv_ref[...])
    lane = lax.broadcasted_iota(jnp.int32, (vv.shape[0], HEAD_SLOT), 1)
    for hd in range(MLA_HEADS):
        v_ref[hd] = _bf(jnp.where(lane < V_HEAD, vv[:, hd * HEAD_SLOT:(hd + 1) * HEAD_SLOT], 1.0))

    c64 = c64_ref[...]
    s64 = s64_ref[...]
    zp_ref[:, ZP_RQ * REC_W:(ZP_RQ + 1) * REC_W] = rq * c64 + rq_rot * s64
    zp_ref[:, ZP_RK * REC_W:(ZP_RK + 1) * REC_W] = (rk * c64 + rk_rot * s64) * (RET_DK ** -0.5)
    zp_ref[:, ZP_RV * REC_W:(ZP_RV + 1) * REC_W] = rv
    zp_ref[:, ZP_RG * REC_W:(ZP_RG + 1) * REC_W] = rg
    lb = lb_ref[...]
    sig = 1.0 / (1.0 + jnp.exp(-gf))
    zp_ref[:, ZP_GQ * REC_W:(ZP_GQ + 1) * REC_W] = _silu(gq)
    zp_ref[:, ZP_GF * REC_W:(ZP_GF + 1) * REC_W] = jnp.log(lb + (1.0 - lb) * sig)
    zp_ref[:, ZP_GK * REC_W:(ZP_GK + 1) * REC_W] = (1.0 - lb) * (1.0 / (1.0 + jnp.exp(gf)))
    zp_ref[:, ZP_GI * REC_W:(ZP_GI + 1) * REC_W] = gi
    zp_ref[:, ZP_GG * REC_W:(ZP_GG + 1) * REC_W] = gg


def _rot_cols(w, n_heads, d):
    k = w.shape[0]
    w4 = w.reshape(k, n_heads, 2, d // 2)
    return jnp.concatenate([-w4[:, :, 1:2], w4[:, :, 0:1]], axis=2).reshape(k, n_heads * d)


def _prep_layer_weights(w_in, w_uq, w_uk, w_uv):
    sizes = (Q_LORA, KV_LORA, QK_ROPE) + (REC_W,) * 8
    offs = [0]
    for s in sizes:
        offs.append(offs[-1] + s)
    cols = [w_in[:, offs[i]:offs[i + 1]] for i in range(len(sizes))]
    cq, ckv, kr, rq, rk, rv, rg, gq, gf, gi, gg = cols
    kdim = w_in.shape[0]
    zeros = lambda n: jnp.zeros((kdim, n), w_in.dtype)
    place = lambda c: jnp.concatenate([zeros(QK_NOPE), c, zeros(HEAD_SLOT - QK_NOPE - QK_ROPE)], axis=1)
    wbig = jnp.concatenate([
        cq, ckv, place(kr), place(_rot_cols(kr, 1, QK_ROPE)),
        rq, _rot_cols(rq, RET_HEADS, RET_DK), rk, _rot_cols(rk, RET_HEADS, RET_DK),
        rv, rg, gq, gf, gi, gg], axis=1)
    wq3 = w_uq.reshape(Q_LORA, MLA_HEADS, QK_NOPE + QK_ROPE)
    nope, ropep = wq3[..., :QK_NOPE], wq3[..., QK_NOPE:]
    rope_rot = jnp.concatenate([-ropep[..., QK_ROPE // 2:], ropep[..., :QK_ROPE // 2]], axis=-1)
    pad = jnp.zeros((Q_LORA, MLA_HEADS, HEAD_SLOT - QK_NOPE - QK_ROPE), w_uq.dtype)
    plain = jnp.concatenate([nope, ropep, pad], axis=-1).reshape(Q_LORA, MLA_HEADS * HEAD_SLOT)
    rotd = jnp.concatenate([jnp.zeros_like(nope), rope_rot, pad], axis=-1).reshape(Q_LORA, MLA_HEADS * HEAD_SLOT)
    wuq = jnp.concatenate([plain, rotd], axis=1)
    wuk = jnp.concatenate([w_uk, jnp.zeros((KV_LORA, MLA_HEADS, HEAD_SLOT - QK_NOPE), w_uk.dtype)],
                          axis=-1).reshape(KV_LORA, MLA_HEADS * HEAD_SLOT)
    wuv = jnp.concatenate([w_uv, jnp.zeros((KV_LORA, MLA_HEADS, HEAD_SLOT - V_HEAD), w_uv.dtype)],
                          axis=-1).reshape(KV_LORA, MLA_HEADS * HEAD_SLOT)
    return _bf(wbig), _bf(wuq), _bf(wuk), _bf(wuv)


def _rope_tables(pos):
    posf = pos.astype(jnp.float32)[:, None]

    def cs(d):
        half = d // 2
        inv = ROPE_BASE ** (-jnp.arange(half, dtype=jnp.float32) / half)
        ang = posf * inv[None, :]
        return jnp.cos(ang), jnp.sin(ang)

    n = pos.shape[0]
    c32, s32 = cs(QK_ROPE)
    tail = jnp.zeros((n, HEAD_SLOT - QK_NOPE - QK_ROPE), jnp.float32)
    cq = jnp.concatenate([jnp.ones((n, QK_NOPE), jnp.float32), c32, c32, tail], axis=1)
    sq = jnp.concatenate([jnp.zeros((n, QK_NOPE), jnp.float32), s32, s32, tail], axis=1)
    c64, s64 = cs(RET_DK)
    c64 = jnp.tile(jnp.concatenate([c64, c64], axis=1), (1, RET_HEADS))
    s64 = jnp.tile(jnp.concatenate([s64, s64], axis=1), (1, RET_HEADS))
    return cq, sq, c64, s64


def _proj_in(x, gattn, wbig, qn, kvn, wuq, wuk, wuv, lb, tabs, tm):
    n = x.shape[0]
    cq_t, sq_t, c64, s64 = tabs
    full = lambda a: pl.BlockSpec(a.shape, lambda i: (0,) * a.ndim)
    row = lambda w: pl.BlockSpec((tm, w), lambda i: (i, 0))
    hrow = lambda nh: pl.BlockSpec((nh, tm, HEAD_SLOT), lambda i: (0, i, 0))
    out_shape = (
        jax.ShapeDtypeStruct((n, KV_LORA), jnp.float32),
        jax.ShapeDtypeStruct((n, QK_ROPE), jnp.float32),
        jax.ShapeDtypeStruct((MLA_HEADS, n, HEAD_SLOT), jnp.bfloat16),
        jax.ShapeDtypeStruct((MLA_HEADS, n, HEAD_SLOT), jnp.bfloat16),
        jax.ShapeDtypeStruct((MLA_HEADS, n, HEAD_SLOT), jnp.bfloat16),
        jax.ShapeDtypeStruct((n, ZP_GROUPS * REC_W), jnp.float32),
    )
    return pl.pallas_call(
        _proj_in_kernel,
        grid=(n // tm,),
        in_specs=[row(D_MODEL), full(gattn), full(wbig), full(qn), full(kvn), full(wuq), full(wuk), full(wuv),
                  full(lb), row(HEAD_SLOT), row(HEAD_SLOT), row(REC_W), row(REC_W)],
        out_specs=(row(KV_LORA), row(QK_ROPE), hrow(MLA_HEADS), hrow(MLA_HEADS), hrow(MLA_HEADS),
                   row(ZP_GROUPS * REC_W)),
        out_shape=out_shape,
        compiler_params=pltpu.CompilerParams(dimension_semantics=("parallel",), vmem_limit_bytes=VMEM_LIMIT),
        name="proj_in",
    )(x, gattn, wbig, qn, kvn, wuq, wuk, wuv, lb, cq_t, sq_t, c64, s64)


def _online_step(s, vb, m, acc):
    m_new = jnp.maximum(m, jnp.max(s, axis=-1, keepdims=True))
    acc = jnp.exp2(m - m_new) * acc + _dot(_bf(jnp.exp2(s - m_new)), vb)
    return m_new, acc


def _attn_prompt_kernel(q_ref, k_ref, v_ref, o_ref, *, blk):
    qi = pl.program_id(2)
    qs = (q_ref[0], q_ref[1])

    def step(j, carry, masked):
        start = pl.multiple_of(j * blk, blk)
        out = []
        for hh in range(2):
            s = _dot_nt(qs[hh], k_ref[hh, pl.ds(start, blk), :])
            if masked:
                row = lax.broadcasted_iota(jnp.int32, (blk, blk), 0)
                col = lax.broadcasted_iota(jnp.int32, (blk, blk), 1)
                s = jnp.where(col <= row, s, NEG_INF)
            out.append(_online_step(s, v_ref[hh, pl.ds(start, blk), :], *carry[hh]))
        return tuple(out)

    init = (jnp.full((blk, 1), NEG_INF, jnp.float32), jnp.zeros((blk, LANES), jnp.float32))
    pairs = qi // 2
    carry = lax.fori_loop(0, pairs, lambda j, c: step(2 * j + 1, step(2 * j, c, False), False), (init, init))
    carry = lax.fori_loop(2 * pairs, qi, lambda j, c: step(j, c, False), carry)
    carry = step(qi, carry, True)
    norm = [acc / pltpu.roll(acc, V_HEAD, axis=1) for _, acc in carry]
    lane = lax.broadcasted_iota(jnp.int32, (blk, LANES), 1)
    o_ref[...] = _bf(jnp.where(lane < V_HEAD, norm[0], pltpu.roll(norm[1], V_HEAD, axis=1)))


def _attn_prompt(q, k, v, batch, seq, blk):
    nq = seq // blk
    return pl.pallas_call(
        functools.partial(_attn_prompt_kernel, blk=blk),
        grid=(batch, MLA_HEADS // 2, nq),
        in_specs=[pl.BlockSpec((2, blk, HEAD_SLOT), lambda b, p, i: (p, b * nq + i, 0)),
                  pl.BlockSpec((2, seq, HEAD_SLOT), lambda b, p, i: (p, b, 0)),
                  pl.BlockSpec((2, seq, LANES), lambda b, p, i: (p, b, 0))],
        out_specs=pl.BlockSpec((blk, LANES), lambda b, p, i: (b * nq + i, p)),
        out_shape=jax.ShapeDtypeStruct((batch * seq, MLA_HEADS * V_HEAD), jnp.bfloat16),
        compiler_params=pltpu.CompilerParams(dimension_semantics=("parallel", "parallel", "arbitrary"),
                                             vmem_limit_bytes=VMEM_LIMIT),
        name="attn_prompt",
    )(q, k, v)


def _attn_sample_kernel(pt_ref, q_ref, wuk_ref, wuvp_ref, ckv_ref, kr_ref, *rest, pages_per_step, td):
    kv_refs = rest[:pages_per_step]
    krt_refs = rest[pages_per_step:2 * pages_per_step]
    o_ref = rest[2 * pages_per_step]
    qlat_ref, qrope_ref, m_ref, l_ref, acc_ref, kall_ref, krt_ref = rest[2 * pages_per_step + 1:]
    j = pl.program_id(1)
    rows = MLA_HEADS * td

    @pl.when(j == 0)
    def _():
        for hd in range(MLA_HEADS):
            qh = q_ref[hd].astype(jnp.float32)
            qlat_ref[hd * td:(hd + 1) * td, :] = _dot_nt(_bf(qh[:, :QK_NOPE]), wuk_ref[hd])
            qrope_ref[hd * td:(hd + 1) * td, :] = qh[:, QK_NOPE:QK_NOPE + QK_ROPE]
        m_ref[...] = jnp.full((rows, 1), NEG_INF, jnp.float32)
        l_ref[...] = jnp.zeros((rows, 1), jnp.float32)
        acc_ref[...] = jnp.zeros((rows, KV_LORA), jnp.float32)

    for r in range(pages_per_step):
        kall_ref[r * PAGE_SIZE:(r + 1) * PAGE_SIZE, :] = _bf(kv_refs[r][...])
        krt_ref[:, r * PAGE_SIZE:(r + 1) * PAGE_SIZE] = _bf(krt_refs[r][...])
    qlat = _bf(qlat_ref[...])
    qrope = _bf(qrope_ref[...])
    kall = kall_ref[...]
    s = _dot_nt(qlat, kall) + _dot(qrope, krt_ref[...])
    m = m_ref[...]
    m_new = jnp.maximum(m, jnp.max(s, axis=-1, keepdims=True))
    alpha = jnp.exp2(m - m_new)
    p = jnp.exp2(s - m_new)
    l = alpha * l_ref[...] + jnp.sum(p, axis=-1, keepdims=True)
    acc = alpha * acc_ref[...] + _dot(_bf(p), kall)
    m_ref[...], l_ref[...], acc_ref[...] = m_new, l, acc

    @pl.when(j == pl.num_programs(1) - 1)
    def _():
        pad = jnp.zeros((PAGE_SIZE - td, KV_LORA), jnp.float32)
        cnew = _bf(jnp.concatenate([ckv_ref[...], pad], axis=0))
        krnew = _bf(jnp.concatenate([kr_ref[...], pad[:, :QK_ROPE]], axis=0))
        sn = _dot_nt(qlat, cnew) + _dot_nt(qrope, krnew)
        row = lax.broadcasted_iota(jnp.int32, (rows, PAGE_SIZE), 0)
        col = lax.broadcasted_iota(jnp.int32, (rows, PAGE_SIZE), 1)
        sn = jnp.where(col <= row % td, sn, NEG_INF)
        m2 = jnp.maximum(m_new, jnp.max(sn, axis=-1, keepdims=True))
        a2 = jnp.exp2(m_new - m2)
        pn = jnp.exp2(sn - m2)
        l2 = a2 * l + jnp.sum(pn, axis=-1, keepdims=True)
        acc2 = a2 * acc + _dot(_bf(pn), cnew)
        olat = _bf(acc2 / l2)
        out = jnp.zeros((td, MLA_HEADS * V_HEAD), jnp.float32)
        for hd in range(MLA_HEADS):
            out = out + _dot(olat[hd * td:(hd + 1) * td, :], wuvp_ref[hd])
        o_ref[...] = out


def _attn_sample(page_table, q, wuk_h, wuv_placed, ckv, kr, cache_kv, cache_krt, layer, row0, dec_batch, td,
                 pages_per_step):
    n_pages = page_table.shape[1]
    steps = n_pages // pages_per_step
    blk0 = row0 // td
    rows = MLA_HEADS * td
    keys = pages_per_step * PAGE_SIZE

    def page_spec(shape, r):
        return pl.BlockSpec((None, None) + shape,
                            lambda b, j, pt, r=r: (layer, pt[b * n_pages + j * pages_per_step + r], 0, 0))

    in_specs = [pl.BlockSpec((MLA_HEADS, td, HEAD_SLOT), lambda b, j, pt: (0, blk0 + b, 0)),
                pl.BlockSpec(wuk_h.shape, lambda b, j, pt: (0, 0, 0)),
                pl.BlockSpec(wuv_placed.shape, lambda b, j, pt: (0, 0, 0)),
                pl.BlockSpec((td, KV_LORA), lambda b, j, pt: (blk0 + b, 0)),
                pl.BlockSpec((td, QK_ROPE), lambda b, j, pt: (blk0 + b, 0))]
    in_specs += [page_spec((PAGE_SIZE, KV_LORA), r) for r in range(pages_per_step)]
    in_specs += [page_spec((QK_ROPE, PAGE_SIZE), r) for r in range(pages_per_step)]
    grid_spec = pltpu.PrefetchScalarGridSpec(
        num_scalar_prefetch=1,
        grid=(dec_batch, steps),
        in_specs=in_specs,
        out_specs=pl.BlockSpec((td, MLA_HEADS * V_HEAD), lambda b, j, pt: (b, 0)),
        scratch_shapes=[pltpu.VMEM((rows, KV_LORA), jnp.float32), pltpu.VMEM((rows, QK_ROPE), jnp.float32),
                        pltpu.VMEM((rows, 1), jnp.float32), pltpu.VMEM((rows, 1), jnp.float32),
                        pltpu.VMEM((rows, KV_LORA), jnp.float32),
                        pltpu.VMEM((keys, KV_LORA), jnp.bfloat16), pltpu.VMEM((QK_ROPE, keys), jnp.bfloat16)])
    return pl.pallas_call(
        functools.partial(_attn_sample_kernel, pages_per_step=pages_per_step, td=td),
        grid_spec=grid_spec,
        out_shape=jax.ShapeDtypeStruct((dec_batch * td, MLA_HEADS * V_HEAD), jnp.float32),
        compiler_params=pltpu.CompilerParams(dimension_semantics=("parallel", "arbitrary"),
                                             vmem_limit_bytes=VMEM_LIMIT),
        name="attn_sample",
    )(page_table.reshape(-1), q, wuk_h, wuv_placed, ckv, kr, *([cache_kv] * pages_per_step),
      *([cache_krt] * pages_per_step))


def _place_wuv(w_uv):
    eye = jnp.eye(MLA_HEADS, dtype=w_uv.dtype)
    placed = jnp.einsum('chv,hg->hcgv', w_uv, eye)
    return _bf(placed.reshape(MLA_HEADS, KV_LORA, MLA_HEADS * V_HEAD))


def _ret_gammas():
    return [1.0 - 2.0 ** (-5.0 - h) for h in range(RET_HEADS)]


def _ret_kernel(*refs, nb, parts, decay_len):
    q_refs, k_refs, v_refs, g_refs = (refs[g * parts:(g + 1) * parts] for g in range(4))
    s0_ref, dmask_ref, qdec_ref, kdec_ref, gn_ref, o_ref, sfin_ref, s_ref = refs[4 * parts:]
    c = pl.program_id(1)

    @pl.when(c == 0)
    def _():
        s_ref[...] = s0_ref[...]

    qdec = qdec_ref[...]
    kdec = kdec_ref[...]
    gn = gn_ref[...]
    for sq in range(parts * nb):
        p, bi = divmod(sq, nb)
        q = q_refs[p][bi]
        k = k_refs[p][bi]
        v = v_refs[p][bi]
        kd = k * kdec
        gate = _silu(g_refs[p][bi])
        for h in range(RET_HEADS):
            sl = slice(h * RET_DK, (h + 1) * RET_DK)
            qh, kh, vh = _bf(q[:, sl]), _bf(k[:, sl]), _bf(v[:, sl])
            a = _dot_nt(qh, kh) * dmask_ref[h]
            s = s_ref[sq, h]
            o = _dot(_bf(a), vh) + _dot(qh, _bf(s)) * qdec[:, sl]
            s_ref[sq, h] = s * (_ret_gammas()[h] ** decay_len) + _dot_tn(_bf(kd[:, sl]), vh)
            oc = o - jnp.mean(o, axis=-1, keepdims=True)
            y = oc * lax.rsqrt(jnp.mean(oc * oc, axis=-1, keepdims=True) + EPS)
            o_ref[sq, :, sl] = _bf(y * gn[:, sl] * gate[:, sl])

    @pl.when(c == pl.num_programs(1) - 1)
    def _():
        sfin_ref[...] = s_ref[...]


def _ret_consts(chunk, decay_len):
    lg = jnp.log(jnp.asarray(_ret_gammas(), jnp.float32))
    idx = jnp.arange(chunk, dtype=jnp.float32)
    diff = idx[:, None] - idx[None, :]
    dmask = jnp.where(diff >= 0, jnp.exp(lg[:, None, None] * jnp.maximum(diff, 0.0)), 0.0)
    qdec = jnp.exp(lg[None, :] * (idx[:, None] + 1.0))
    kdec = jnp.exp(lg[None, :] * (decay_len - 1.0 - idx[:, None]))
    rep = lambda a: jnp.repeat(a, RET_DK, axis=1)
    return dmask, rep(qdec), rep(kdec)


def _rec_specs(nb, parts, chunk, nch, groups):
    return [pl.BlockSpec((nb, chunk, REC_W), lambda b, c, g=g, p=p: ((b * parts + p) * nch + c, 0, g))
            for g in groups for p in range(parts)]


def _rec_call(kernel_fn, name, groups, zp3, s0, extra, nb, parts, nch):
    chunk = zp3.shape[1]
    seqs = s0.shape[0]
    per_step = parts * nb
    full = lambda a: pl.BlockSpec(a.shape, lambda b, c: (0,) * a.ndim)
    st = pl.BlockSpec((per_step,) + s0.shape[1:], lambda b, c: (b, 0, 0, 0))
    return pl.pallas_call(
        kernel_fn,
        grid=(seqs // per_step, nch),
        in_specs=_rec_specs(nb, parts, chunk, nch, groups) + [st] + [full(a) for a in extra],
        out_specs=(pl.BlockSpec((per_step, None, chunk, REC_W), lambda b, c: (b, c, 0, 0)), st),
        out_shape=(jax.ShapeDtypeStruct((seqs, nch, chunk, REC_W), jnp.bfloat16),
                   jax.ShapeDtypeStruct(s0.shape, jnp.float32)),
        scratch_shapes=[pltpu.VMEM((per_step,) + s0.shape[1:], jnp.float32)],
        compiler_params=pltpu.CompilerParams(dimension_semantics=("parallel", "arbitrary"),
                                             vmem_limit_bytes=VMEM_LIMIT),
        name=name,
    )(*([zp3] * (len(groups) * parts)), s0, *extra)


def _retention(zp3, s0, gn, nb, parts, nch, decay_len):
    consts = _ret_consts(zp3.shape[1], decay_len)
    return _rec_call(functools.partial(_ret_kernel, nb=nb, parts=parts, decay_len=decay_len), "retention",
                     (ZP_RQ, ZP_RK, ZP_RV, ZP_RG), zp3, s0, consts + (gn,), nb, parts, nch)


def _hgrn_sum_matrix(chunk):
    import numpy as np
    t = np.arange(chunk)[:, None]
    r = np.arange(chunk)[None, :]
    blocks = [r <= t, r > t]
    m = 2
    while m <= chunk:
        half = m // 2
        mid = (t // m) * m + half
        upper = (t % m) >= half
        blocks.append(upper & (r >= mid) & (r <= t))
        blocks.append(~upper & (r > t) & (r <= mid - 1))
        m *= 2
    return jnp.asarray(np.concatenate(blocks, axis=0).astype(np.float32), jnp.bfloat16)


def _hgrn_kernel(*refs, nb, parts, chunk):
    q_refs, f_refs, k_refs, v_refs, g_refs = (refs[g * parts:(g + 1) * parts] for g in range(5))
    s0_ref, msum_ref, gn_ref, o_ref, sfin_ref, st_ref = refs[5 * parts:]
    c = pl.program_id(1)
    nlev = chunk.bit_length() - 1

    @pl.when(c == 0)
    def _():
        for bi in range(parts * nb):
            for h in range(HG_HEADS):
                st_ref[bi, h] = s0_ref[bi, h].T

    gn = gn_ref[...]
    rowi = lax.broadcasted_iota(jnp.int32, (chunk, REC_W), 0)
    r2 = lax.broadcasted_iota(jnp.int32, (chunk, chunk), 0)
    c2 = lax.broadcasted_iota(jnp.int32, (chunk, chunk), 1)
    for sq in range(parts * nb):
        p, bi = divmod(sq, nb)
        q = q_refs[p][bi]
        k = k_refs[p][bi]
        v = v_refs[p][bi]
        gate = _silu(g_refs[p][bi])
        f1, f2, f3 = _split3(f_refs[p][bi])
        sums = _dot(msum_ref[...], jnp.concatenate([f1, f2, f3], axis=1))
        sums = sums[:, :REC_W] + sums[:, REC_W:2 * REC_W] + sums[:, 2 * REC_W:]
        part = lambda i: sums[i * chunk:(i + 1) * chunk]
        b = part(0)
        qb = _bf(q * jnp.exp(b))
        ks = _bf(k * jnp.exp(part(1)))
        e_last = jnp.exp(b[chunk - 1:chunk, :])
        vb = _bf(v)
        qs, kls = [_bf(q)], [_bf(k)]
        for lev in range(1, nlev + 1):
            upper = (rowi & (1 << (lev - 1))) != 0
            qs.append(_bf(jnp.where(upper, q * jnp.exp(part(2 * lev)), 0.0)))
            kls.append(_bf(jnp.where(upper, 0.0, k * jnp.exp(part(2 * lev + 1)))))
        for h in range(HG_HEADS):
            sl = slice(h * HG_DK, (h + 1) * HG_DK)
            a = jnp.where(r2 == c2, _dot_nt(qs[0][:, sl], kls[0][:, sl]), 0.0)
            for lev in range(1, nlev + 1):
                same = (r2 >> lev) == (c2 >> lev)
                a = a + jnp.where(same, _dot_nt(qs[lev][:, sl], kls[lev][:, sl]), 0.0)
            st = st_ref[sq, h]
            o = _dot(_bf(a), vb[:, sl]) + _dot_nt(qb[:, sl], _bf(st))
            st_ref[sq, h] = st * e_last[:, sl] + _dot_tn(vb[:, sl], ks[:, sl])
            y = o * lax.rsqrt(jnp.mean(o * o, axis=-1, keepdims=True) + EPS)
            o_ref[sq, :, sl] = _bf(y * gn[:, sl] * gate[:, sl])

    @pl.when(c == pl.num_programs(1) - 1)
    def _():
        for bi in range(parts * nb):
            for h in range(HG_HEADS):
                sfin_ref[bi, h] = st_ref[bi, h].T


def _hgrn(zp3, s0, gn, nb, parts, nch):
    chunk = zp3.shape[1]
    return _rec_call(functools.partial(_hgrn_kernel, nb=nb, parts=parts, chunk=chunk), "hgrn2",
                     (ZP_GQ, ZP_GF, ZP_GK, ZP_GI, ZP_GG), zp3, s0, (_hgrn_sum_matrix(chunk), gn), nb, parts, nch)


def _dot3(ah, al, bh, bl, dot):
    return dot(ah, bh) + dot(al, bh) + dot(ah, bl)


def _split2(x):
    hi = _bf(x)
    return hi, _bf(x - hi.astype(jnp.float32))


def _out_proj_kernel(x_ref, omla_ref, oret_ref, ohg_ref, wout_ref, gffn_ref, wqh_ref, wql_ref, kh_ref, kl_ref,
                     x1_ref, h2t_ref, st_ref):
    mix = jnp.concatenate([omla_ref[...], oret_ref[...], ohg_ref[...]], axis=1)
    x1 = x_ref[...] + _dot(mix, wout_ref[...])
    x1_ref[...] = x1
    h2 = _rms(x1, gffn_ref[...])
    h2t_ref[...] = _bf(h2.T)
    hh, hl = _split2(h2)
    q = _dot3(hh, hl, wqh_ref[...], wql_ref[...], _dot)
    for p in range(2 * PEER_HEADS):
        qh, ql = _split2(q[:, p * PEER_HALF:(p + 1) * PEER_HALF])
        sp = _dot3(kh_ref[p], kl_ref[p], qh, ql, _dot_nt)
        for c in range(sp.shape[1] // LANES):
            st_ref[p, c] = sp[:, c * LANES:(c + 1) * LANES]


def _out_proj(x, omla, oret, ohg, wout, gffn, wqh, wql, kh, kl, tm):
    n = x.shape[0]
    full = lambda a: pl.BlockSpec(a.shape, lambda i: (0,) * a.ndim)
    row = lambda w: pl.BlockSpec((tm, w), lambda i: (i, 0))
    return pl.pallas_call(
        _out_proj_kernel,
        grid=(n // tm,),
        in_specs=[row(D_MODEL), row(MLA_HEADS * V_HEAD), row(REC_W), row(REC_W), full(wout), full(gffn), full(wqh),
                  full(wql), full(kh), full(kl)],
        out_specs=(row(D_MODEL), pl.BlockSpec((D_MODEL, tm), lambda i: (0, i)),
                   pl.BlockSpec((2 * PEER_HEADS, tm // LANES, PEER_NKEYS, LANES), lambda i: (0, i, 0, 0))),
        out_shape=(jax.ShapeDtypeStruct((n, D_MODEL), jnp.float32),
                   jax.ShapeDtypeStruct((D_MODEL, n), jnp.bfloat16),
                   jax.ShapeDtypeStruct((2 * PEER_HEADS, n // LANES, PEER_NKEYS, LANES), jnp.float32)),
        compiler_params=pltpu.CompilerParams(dimension_semantics=("parallel",), vmem_limit_bytes=VMEM_LIMIT),
        name="out_proj_peer_scores",
    )(x, omla, oret, ohg, wout, gffn, wqh, wql, kh, kl)


def _extract_top(x, count, on_value):
    rowf = lax.broadcasted_iota(jnp.int32, x.shape, 1).astype(jnp.float32)
    for r in range(count):
        m = jnp.max(x, axis=1, keepdims=True)
        first = jnp.min(jnp.where(x == m, rowf, float(x.shape[1])), axis=1, keepdims=True)
        hit = rowf == first
        on_value(r, m, hit)
        x = jnp.where(hit, NEG_INF, x)
    return x


SUB = 8


G_S2, G_E2, G_THETA, G_COEF = range(4)


def _peer_select_kernel(s_ref, gates_ref, v1_ref, v2_ref, cand_ref):
    s1 = s_ref[:, 0]
    s2 = s_ref[:, 1]

    def keep(ref):
        def on_value(r, m, hit):
            ref[:, r:r + 1, :] = m
        return on_value

    _extract_top(s1, PEER_TOPK + 1, keep(v1_ref))
    _extract_top(s2, PEER_TOPK + 1, keep(v2_ref))
    for r1 in range(SUB):
        cand_ref[:, r1 * SUB:(r1 + 1) * SUB, :] = v1_ref[:, r1:r1 + 1, :] + v2_ref[:, 0:SUB, :]
    cand_ref[:, SUB * SUB:SUB * SUB + SUB, :] = v1_ref[:, 0:1, :] + v2_ref[:, SUB:2 * SUB, :]
    cand_ref[:, SUB * SUB + SUB:SUB * SUB + 2 * SUB, :] = v1_ref[:, SUB:2 * SUB, :] + v2_ref[:, 0:1, :]
    best = v1_ref[:, 0:1, :] + v2_ref[:, 0:1, :]
    picked = []
    rest = _extract_top(cand_ref[...], PEER_TOPK, lambda r, m, hit: picked.append(m))
    z = jnp.zeros_like(best)
    for m in picked:
        z = z + jnp.exp(m - best)
    nxt = jnp.max(rest, axis=1, keepdims=True)
    nxt = jnp.maximum(nxt, v1_ref[:, PEER_TOPK:PEER_TOPK + 1, :] + v2_ref[:, 0:1, :])
    nxt = jnp.maximum(nxt, v1_ref[:, 0:1, :] + v2_ref[:, PEER_TOPK:PEER_TOPK + 1, :])
    tau = 0.5 * (picked[-1] + nxt)
    gates_ref[:, G_S2] = s2
    gates_ref[:, G_E2] = jnp.exp(s2 - v2_ref[:, 0:1, :])
    gates_ref[:, G_THETA] = tau - s1
    gates_ref[:, G_COEF] = jnp.exp(s1 - v1_ref[:, 0:1, :]) * (1.0 / z)


def _peer_select(st):
    nt = st.shape[1]
    s5 = st.reshape(PEER_HEADS, 2, nt, PEER_NKEYS, LANES)
    vrows = SUB * ((PEER_TOPK + 1 + SUB - 1) // SUB)
    return pl.pallas_call(
        _peer_select_kernel,
        grid=(nt,),
        in_specs=[pl.BlockSpec((PEER_HEADS, 2, None, PEER_NKEYS, LANES), lambda t: (0, 0, t, 0, 0))],
        out_specs=pl.BlockSpec((PEER_HEADS, None, 4, PEER_NKEYS, LANES), lambda t: (0, t, 0, 0, 0)),
        out_shape=jax.ShapeDtypeStruct((PEER_HEADS, nt, 4, PEER_NKEYS, LANES), jnp.float32),
        scratch_shapes=[pltpu.VMEM((PEER_HEADS, vrows, LANES), jnp.float32),
                        pltpu.VMEM((PEER_HEADS, vrows, LANES), jnp.float32),
                        pltpu.VMEM((PEER_HEADS, SUB * SUB + 2 * SUB, LANES), jnp.float32)],
        compiler_params=pltpu.CompilerParams(dimension_semantics=("parallel",), vmem_limit_bytes=VMEM_LIMIT),
        name="peer_select",
    )(s5)


def _peer_expert_kernel(h2t_ref, u_ref, vt_ref, gates_ref, x1_ref, gfin_ref, o_ref, acc_ref, p_ref, at_ref, *,
                        ni, final_norm):
    step = pl.program_id(1)

    @pl.when(step == 0)
    def _():
        acc_ref[...] = jnp.zeros_like(acc_ref)

    tm = h2t_ref.shape[1]
    strips = tm // LANES
    at_ref[...] = _dot(u_ref[...], h2t_ref[...])

    def across(h, plane, rows):
        return jnp.concatenate([gates_ref[h, c, plane, rows, :] for c in range(strips)], axis=1)

    every = slice(None)
    for ib in range(ni):
        row_i = pl.ds(step * ni + ib, 1)
        w = None
        for h in range(PEER_HEADS):
            sel = across(h, G_S2, every) >= across(h, G_THETA, row_i)
            term = jnp.where(sel, across(h, G_E2, every) * across(h, G_COEF, row_i), 0.0)
            w = term if w is None else w + term
        rows = slice(ib * PEER_NKEYS, (ib + 1) * PEER_NKEYS)
        a = at_ref[rows, :]
        gelu = 0.5 * a * (1.0 + lax.erf(a * (2.0 ** -0.5)))
        p_ref[rows, :] = _bf(gelu * w)
    acc_ref[...] += _dot(vt_ref[...], p_ref[...])

    @pl.when(step == pl.num_programs(1) - 1)
    def _():
        x2 = x1_ref[...] + acc_ref[...].T
        o_ref[...] = _rms(x2, gfin_ref[...]) if final_norm else x2


def _peer_experts(h2t, u, vt, gates, x1, gfin, tm, ni, final_norm):
    n = x1.shape[0]
    et = ni * PEER_NKEYS
    blocks = PEER_NKEYS // ni
    return pl.pallas_call(
        functools.partial(_peer_expert_kernel, ni=ni, final_norm=final_norm),
        grid=(n // tm, blocks),
        in_specs=[pl.BlockSpec((D_MODEL, tm), lambda t, i: (0, t)),
                  pl.BlockSpec((et, D_MODEL), lambda t, i: (i, 0)),
                  pl.BlockSpec((None, D_MODEL, et), lambda t, i: (i, 0, 0)),
                  pl.BlockSpec((PEER_HEADS, tm // LANES, 4, PEER_NKEYS, LANES), lambda t, i: (0, t, 0, 0, 0)),
                  pl.BlockSpec((tm, D_MODEL), lambda t, i: (t, 0)),
                  pl.BlockSpec(gfin.shape, lambda t, i: (0, 0))],
        out_specs=pl.BlockSpec((tm, D_MODEL), lambda t, i: (t, 0)),
        out_shape=jax.ShapeDtypeStruct((n, D_MODEL), jnp.float32),
        scratch_shapes=[pltpu.VMEM((D_MODEL, tm), jnp.float32), pltpu.VMEM((et, tm), jnp.bfloat16),
                        pltpu.VMEM((et, tm), jnp.float32)],
        compiler_params=pltpu.CompilerParams(dimension_semantics=("parallel", "arbitrary"),
                                             vmem_limit_bytes=VMEM_LIMIT),
        name="peer_experts",
    )(h2t, u, vt, gates, x1, gfin)


TOKEN_TILE = 256
ATTN_BLOCK = 1024
REC_CHUNK = 128
SAMPLE_CHUNK = 16
SAMPLE_SEQS_PER_STEP = 8
PAGES_PER_STEP = 64
EXPERT_TOKEN_TILE = 512
EXPERT_ROWS_PER_STEP = 8


def kernel(x_prompt, x_sample, cache_kv_latent, cache_k_rope, state_retention, state_hgrn, page_table,
           w_in, q_norm, w_uq, kv_norm, w_uk, w_uv, ret_norm, hg_norm, hg_lower_bounds, w_out,
           attn_norm, ffn_norm, final_norm, peer_wq, peer_keys, peer_u, peer_v):
    batch, seq, d = x_prompt.shape
    dec_batch, td, _ = x_sample.shape
    depth = w_in.shape[0]
    n_p, n_s = batch * seq, dec_batch * td
    past_len = page_table.shape[1] * PAGE_SIZE
    f32 = jnp.float32

    x = jnp.concatenate([x_prompt.reshape(n_p, d), x_sample.reshape(n_s, d)], axis=0)
    pos = jnp.concatenate([jnp.tile(jnp.arange(seq, dtype=jnp.int32), batch),
                           jnp.tile(past_len + jnp.arange(td, dtype=jnp.int32), dec_batch)])
    tabs = _rope_tables(pos)
    cache_krt = jnp.swapaxes(cache_k_rope, 2, 3)
    pages_per_step = math.gcd(PAGES_PER_STEP, page_table.shape[1])
    lb_soft = jax.nn.softmax(hg_lower_bounds.astype(f32), axis=0)
    lbs = jnp.cumsum(lb_soft, axis=0) - lb_soft[0]
    zeros_state = jnp.zeros((batch, RET_HEADS, RET_DK, RET_DV), f32)
    nch = seq // REC_CHUNK
    row2 = lambda a: a.reshape(1, -1)

    outs = {k: [] for k in ("p_ckv", "p_kr", "p_ret", "p_hg", "s_ckv", "s_kr", "s_ret", "s_hg")}
    for l in range(depth):
        wbig, wuq, wuk, wuv = _prep_layer_weights(w_in[l], w_uq[l], w_uk[l], w_uv[l])
        ckv, kr, q, k, v, zp = _proj_in(x, row2(attn_norm[l]), wbig, row2(q_norm[l]), row2(kv_norm[l]), wuq, wuk,
                                        wuv, row2(lbs[l]), tabs, TOKEN_TILE)
        o_p = _attn_prompt(q, k, v, batch, seq, ATTN_BLOCK)
        o_s = _attn_sample(page_table, q, _bf(jnp.transpose(w_uk[l], (1, 0, 2))), _place_wuv(w_uv[l]), ckv, kr,
                           cache_kv_latent, cache_krt, l, n_p, dec_batch, td, pages_per_step)
        o_mla = jnp.concatenate([o_p, _bf(o_s)], axis=0)
        zp_p = zp.reshape(-1, REC_CHUNK, zp.shape[-1])
        zp_s = jnp.pad(zp[n_p:].reshape(dec_batch, td, -1), ((0, 0), (0, SAMPLE_CHUNK - td), (0, 0)))
        gn_ret, gn_hg = row2(ret_norm[l]), row2(hg_norm[l])
        ret_p, ret_sp = _retention(zp_p, zeros_state, gn_ret, 1, batch, nch, REC_CHUNK)
        ret_s, ret_ss = _retention(zp_s, state_retention[l], gn_ret, SAMPLE_SEQS_PER_STEP, 1, 1, td)
        hg_p, hg_sp = _hgrn(zp_p, zeros_state, gn_hg, 1, batch, nch)
        hg_s, hg_ss = _hgrn(zp_s, state_hgrn[l], gn_hg, SAMPLE_SEQS_PER_STEP, 1, 1)
        join = lambda p, s: jnp.concatenate([p.reshape(n_p, REC_W), s[:, 0, :td].reshape(n_s, REC_W)], axis=0)
        o_ret, o_hg = join(ret_p, ret_s), join(hg_p, hg_s)
        wqh, wql = _split2(peer_wq[l])
        kh, kl = _split2(peer_keys[l].reshape(2 * PEER_HEADS, PEER_NKEYS, PEER_HALF))
        x1, h2t, st = _out_proj(x, o_mla, o_ret, o_hg, _bf(w_out[l]), row2(ffn_norm[l]), wqh, wql, kh, kl,
                                TOKEN_TILE)
        gates = _peer_select(st)
        et = EXPERT_ROWS_PER_STEP * PEER_NKEYS
        vt = _bf(jnp.transpose(peer_v[l].reshape(-1, et, d), (0, 2, 1)))
        x = _peer_experts(h2t, _bf(peer_u[l]), vt, gates, x1, row2(final_norm), EXPERT_TOKEN_TILE,
                          EXPERT_ROWS_PER_STEP, l == depth - 1)
        outs["p_ckv"].append(ckv[:n_p].reshape(batch, seq, KV_LORA))
        outs["p_kr"].append(kr[:n_p].reshape(batch, seq, QK_ROPE))
        outs["p_ret"].append(ret_sp)
        outs["p_hg"].append(hg_sp)
        outs["s_ckv"].append(ckv[n_p:].reshape(dec_batch, td, KV_LORA))
        outs["s_kr"].append(kr[n_p:].reshape(dec_batch, td, QK_ROPE))
        outs["s_ret"].append(ret_ss)
        outs["s_hg"].append(hg_ss)
    y_prompt = x[:n_p].reshape(batch, seq, d)
    y_sample = x[n_p:].reshape(dec_batch, td, d)
    return (y_prompt, y_sample) + tuple(jnp.stack(outs[k]) for k in
                                        ("p_ckv", "p_kr", "p_ret", "p_hg", "s_ckv", "s_kr", "s_ret", "s_hg"))
```

```python
import functools
import math

import jax
import jax.numpy as jnp
from jax import lax
from jax.experimental import pallas as pl
from jax.experimental.pallas import tpu as pltpu

D_MODEL = 1024
MLA_HEADS = 8
QK_NOPE = 64
QK_ROPE = 32
V_HEAD = 64
Q_LORA = 256
KV_LORA = 128
RET_HEADS = 4
RET_DK = 64
RET_DV = 64
HG_HEADS = 4
HG_DK = 64
HG_DV = 64
PEER_HEADS = 8
PEER_NKEYS = 128
PEER_HALF = 128
PEER_TOPK = 16
ROPE_BASE = 10000.0
EPS = 1e-6
PAGE_SIZE = 128

LANES = 128
HEAD_SLOT = 128
REC_W = 256
NEG_INF = float("-inf")
VMEM_LIMIT = 56 * 1024 * 1024

ZP_RQ, ZP_RK, ZP_RV, ZP_RG, ZP_GQ, ZP_GF, ZP_GK, ZP_GI, ZP_GG = range(9)
ZP_GROUPS = 9


def _bf(x):
    return x.astype(jnp.bfloat16)


def _dot(a, b):
    return jnp.dot(a, b, preferred_element_type=jnp.float32)


def _dot_nt(a, b):
    return lax.dot_general(a, b, (((1,), (1,)), ((), ())), preferred_element_type=jnp.float32)


def _dot_tn(a, b):
    return lax.dot_general(a, b, (((0,), (0,)), ((), ())), preferred_element_type=jnp.float32)


def _rms(x, g):
    return x * lax.rsqrt(jnp.mean(x * x, axis=-1, keepdims=True) + EPS) * g


def _silu(x):
    return x * (1.0 / (1.0 + jnp.exp(-x)))


def _split3(x):
    x1 = _bf(x)
    r1 = x - x1.astype(jnp.float32)
    x2 = _bf(r1)
    x3 = _bf(r1 - x2.astype(jnp.float32))
    return x1, x2, x3


def _proj_in_kernel(x_ref, gattn_ref, wbig_ref, qn_ref, kvn_ref, wuq_ref, wuk_ref, wuv_ref, lb_ref,
                    cq_ref, sq_ref, c64_ref, s64_ref,
                    ckv_ref, kr_ref, q_ref, k_ref, v_ref, zp_ref):
    scale = (QK_NOPE + QK_ROPE) ** -0.5 * math.log2(math.e)
    h = _bf(_rms(x_ref[...], gattn_ref[...]))
    z = _dot(h, wbig_ref[...])
    o = 0
    cq = z[:, o:o + Q_LORA]; o += Q_LORA
    ckv = z[:, o:o + KV_LORA]; o += KV_LORA
    krp = z[:, o:o + HEAD_SLOT]; o += HEAD_SLOT
    krp_rot = z[:, o:o + HEAD_SLOT]; o += HEAD_SLOT
    groups = []
    for _ in range(10):
        groups.append(z[:, o:o + REC_W]); o += REC_W
    rq, rq_rot, rk, rk_rot, rv, rg, gq, gf, gi, gg = groups

    cq_t = cq_ref[...]
    sq_t = sq_ref[...]
    cqn = _bf(_rms(cq, qn_ref[...]))
    zq = _dot(cqn, wuq_ref[...])
    hw = MLA_HEADS * HEAD_SLOT
    for hd in range(MLA_HEADS):
        a = zq[:, hd * HEAD_SLOT:(hd + 1) * HEAD_SLOT]
        b = zq[:, hw + hd * HEAD_SLOT:hw + (hd + 1) * HEAD_SLOT]
        q_ref[hd] = _bf((a * cq_t + b * sq_t) * scale)
    ckvn = _rms(ckv, kvn_ref[...])
    ckv_ref[...] = ckvn
    kr_placed = krp * cq_t + krp_rot * sq_t
    kr_ref[...] = kr_placed[:, QK_NOPE:QK_NOPE + QK_ROPE]
    ckvb = _bf(ckvn)
    kn = _dot(ckvb, wuk_ref[...])
    for hd in range(MLA_HEADS):
        k_ref[hd] = _bf(kn[:, hd * HEAD_SLOT:(hd + 1) * HEAD_SLOT] + kr_placed)
    vv = _dot(ckvb, wuv_ref[...])
    lane = lax.broadcasted_iota(jnp.int32, (vv.shape[0], HEAD_SLOT), 1)
    for hd in range(MLA_HEADS):
        v_ref[hd] = _bf(jnp.where(lane < V_HEAD, vv[:, hd * HEAD_SLOT:(hd + 1) * HEAD_SLOT], 1.0))

    c64 = c64_ref[...]
    s64 = s64_ref[...]
    zp_ref[:, ZP_RQ * REC_W:(ZP_RQ + 1) * REC_W] = rq * c64 + rq_rot * s64
    zp_ref[:, ZP_RK * REC_W:(ZP_RK + 1) * REC_W] = (rk * c64 + rk_rot * s64) * (RET_DK ** -0.5)
    zp_ref[:, ZP_RV * REC_W:(ZP_RV + 1) * REC_W] = rv
    zp_ref[:, ZP_RG * REC_W:(ZP_RG + 1) * REC_W] = rg
    lb = lb_ref[...]
    sig = 1.0 / (1.0 + jnp.exp(-gf))
    zp_ref[:, ZP_GQ * REC_W:(ZP_GQ + 1) * REC_W] = _silu(gq)
    zp_ref[:, ZP_GF * REC_W:(ZP_GF + 1) * REC_W] = jnp.log(lb + (1.0 - lb) * sig)
    zp_ref[:, ZP_GK * REC_W:(ZP_GK + 1) * REC_W] = (1.0 - lb) * (1.0 / (1.0 + jnp.exp(gf)))
    zp_ref[:, ZP_GI * REC_W:(ZP_GI + 1) * REC_W] = gi
    zp_ref[:, ZP_GG * REC_W:(ZP_GG + 1) * REC_W] = gg


def _rot_cols(w, n_heads, d):
    k = w.shape[0]
    w4 = w.reshape(k, n_heads, 2, d // 2)
    return jnp.concatenate([-w4[:, :, 1:2], w4[:, :, 0:1]], axis=2).reshape(k, n_heads * d)


def _prep_layer_weights(w_in, w_uq, w_uk, w_uv):
    sizes = (Q_LORA, KV_LORA, QK_ROPE) + (REC_W,) * 8
    offs = [0]
    for s in sizes:
        offs.append(offs[-1] + s)
    cols = [w_in[:, offs[i]:offs[i + 1]] for i in range(len(sizes))]
    cq, ckv, kr, rq, rk, rv, rg, gq, gf, gi, gg = cols
    kdim = w_in.shape[0]
    zeros = lambda n: jnp.zeros((kdim, n), w_in.dtype)
    place = lambda c: jnp.concatenate([zeros(QK_NOPE), c, zeros(HEAD_SLOT - QK_NOPE - QK_ROPE)], axis=1)
    wbig = jnp.concatenate([
        cq, ckv, place(kr), place(_rot_cols(kr, 1, QK_ROPE)),
        rq, _rot_cols(rq, RET_HEADS, RET_DK), rk, _rot_cols(rk, RET_HEADS, RET_DK),
        rv, rg, gq, gf, gi, gg], axis=1)
    wq3 = w_uq.reshape(Q_LORA, MLA_HEADS, QK_NOPE + QK_ROPE)
    nope, ropep = wq3[..., :QK_NOPE], wq3[..., QK_NOPE:]
    rope_rot = jnp.concatenate([-ropep[..., QK_ROPE // 2:], ropep[..., :QK_ROPE // 2]], axis=-1)
    pad = jnp.zeros((Q_LORA, MLA_HEADS, HEAD_SLOT - QK_NOPE - QK_ROPE), w_uq.dtype)
    plain = jnp.concatenate([nope, ropep, pad], axis=-1).reshape(Q_LORA, MLA_HEADS * HEAD_SLOT)
    rotd = jnp.concatenate([jnp.zeros_like(nope), rope_rot, pad], axis=-1).reshape(Q_LORA, MLA_HEADS * HEAD_SLOT)
    wuq = jnp.concatenate([plain, rotd], axis=1)
    wuk = jnp.concatenate([w_uk, jnp.zeros((KV_LORA, MLA_HEADS, HEAD_SLOT - QK_NOPE), w_uk.dtype)],
                          axis=-1).reshape(KV_LORA, MLA_HEADS * HEAD_SLOT)
    wuv = jnp.concatenate([w_uv, jnp.zeros((KV_LORA, MLA_HEADS, HEAD_SLOT - V_HEAD), w_uv.dtype)],
                          axis=-1).reshape(KV_LORA, MLA_HEADS * HEAD_SLOT)
    return _bf(wbig), _bf(wuq), _bf(wuk), _bf(wuv)


def _rope_tables(pos):
    posf = pos.astype(jnp.float32)[:, None]

    def cs(d):
        half = d // 2
        inv = ROPE_BASE ** (-jnp.arange(half, dtype=jnp.float32) / half)
        ang = posf * inv[None, :]
        return jnp.cos(ang), jnp.sin(ang)

    n = pos.shape[0]
    c32, s32 = cs(QK_ROPE)
    tail = jnp.zeros((n, HEAD_SLOT - QK_NOPE - QK_ROPE), jnp.float32)
    cq = jnp.concatenate([jnp.ones((n, QK_NOPE), jnp.float32), c32, c32, tail], axis=1)
    sq = jnp.concatenate([jnp.zeros((n, QK_NOPE), jnp.float32), s32, s32, tail], axis=1)
    c64, s64 = cs(RET_DK)
    c64 = jnp.tile(jnp.concatenate([c64, c64], axis=1), (1, RET_HEADS))
    s64 = jnp.tile(jnp.concatenate([s64, s64], axis=1), (1, RET_HEADS))
    return cq, sq, c64, s64


def _proj_in(x, gattn, wbig, qn, kvn, wuq, wuk, wuv, lb, tabs, tm):
    n = x.shape[0]
    cq_t, sq_t, c64, s64 = tabs
    full = lambda a: pl.BlockSpec(a.shape, lambda i: (0,) * a.ndim)
    row = lambda w: pl.BlockSpec((tm, w), lambda i: (i, 0))
    hrow = lambda nh: pl.BlockSpec((nh, tm, HEAD_SLOT), lambda i: (0, i, 0))
    out_shape = (
        jax.ShapeDtypeStruct((n, KV_LORA), jnp.float32),
        jax.ShapeDtypeStruct((n, QK_ROPE), jnp.float32),
        jax.ShapeDtypeStruct((MLA_HEADS, n, HEAD_SLOT), jnp.bfloat16),
        jax.ShapeDtypeStruct((MLA_HEADS, n, HEAD_SLOT), jnp.bfloat16),
        jax.ShapeDtypeStruct((MLA_HEADS, n, HEAD_SLOT), jnp.bfloat16),
        jax.ShapeDtypeStruct((n, ZP_GROUPS * REC_W), jnp.float32),
    )
    return pl.pallas_call(
        _proj_in_kernel,
        grid=(n // tm,),
        in_specs=[row(D_MODEL), full(gattn), full(wbig), full(qn), full(kvn), full(wuq), full(wuk), full(wuv),
                  full(lb), row(HEAD_SLOT), row(HEAD_SLOT), row(REC_W), row(REC_W)],
        out_specs=(row(KV_LORA), row(QK_ROPE), hrow(MLA_HEADS), hrow(MLA_HEADS), hrow(MLA_HEADS),
                   row(ZP_GROUPS * REC_W)),
        out_shape=out_shape,
        compiler_params=pltpu.CompilerParams(dimension_semantics=("parallel",), vmem_limit_bytes=VMEM_LIMIT),
        name="proj_in",
    )(x, gattn, wbig, qn, kvn, wuq, wuk, wuv, lb, cq_t, sq_t, c64, s64)


def _online_step(s, vb, m, acc):
    m_new = jnp.maximum(m, jnp.max(s, axis=-1, keepdims=True))
    acc = jnp.exp2(m - m_new) * acc + _dot(_bf(jnp.exp2(s - m_new)), vb)
    return m_new, acc


def _attn_prompt_kernel(q_ref, k_ref, v_ref, o_ref, *, blk):
    qi = pl.program_id(2)
    qs = (q_ref[0], q_ref[1])

    def step(j, carry, masked):
        start = pl.multiple_of(j * blk, blk)
        out = []
        for hh in range(2):
            s = _dot_nt(qs[hh], k_ref[hh, pl.ds(start, blk), :])
            if masked:
                row = lax.broadcasted_iota(jnp.int32, (blk, blk), 0)
                col = lax.broadcasted_iota(jnp.int32, (blk, blk), 1)
                s = jnp.where(col <= row, s, NEG_INF)
            out.append(_online_step(s, v_ref[hh, pl.ds(start, blk), :], *carry[hh]))
        return tuple(out)

    init = (jnp.full((blk, 1), NEG_INF, jnp.float32), jnp.zeros((blk, LANES), jnp.float32))
    pairs = qi // 2
    carry = lax.fori_loop(0, pairs, lambda j, c: step(2 * j + 1, step(2 * j, c, False), False), (init, init))
    carry = lax.fori_loop(2 * pairs, qi, lambda j, c: step(j, c, False), carry)
    carry = step(qi, carry, True)
    norm = [acc / pltpu.roll(acc, V_HEAD, axis=1) for _, acc in carry]
    lane = lax.broadcasted_iota(jnp.int32, (blk, LANES), 1)
    o_ref[...] = _bf(jnp.where(lane < V_HEAD, norm[0], pltpu.roll(norm[1], V_HEAD, axis=1)))


def _attn_prompt(q, k, v, batch, seq, blk):
    nq = seq // blk
    return pl.pallas_call(
        functools.partial(_attn_prompt_kernel, blk=blk),
        grid=(batch, MLA_HEADS // 2, nq),
        in_specs=[pl.BlockSpec((2, blk, HEAD_SLOT), lambda b, p, i: (p, b * nq + i, 0)),
                  pl.BlockSpec((2, seq, HEAD_SLOT), lambda b, p, i: (p, b, 0)),
                  pl.BlockSpec((2, seq, LANES), lambda b, p, i: (p, b, 0))],
        out_specs=pl.BlockSpec((blk, LANES), lambda b, p, i: (b * nq + i, p)),
        out_shape=jax.ShapeDtypeStruct((batch * seq, MLA_HEADS * V_HEAD), jnp.bfloat16),
        compiler_params=pltpu.CompilerParams(dimension_semantics=("parallel", "parallel", "arbitrary"),
                                             vmem_limit_bytes=VMEM_LIMIT),
        name="attn_prompt",
    )(q, k, v)


def _attn_sample_kernel(pt_ref, q_ref, wuk_ref, wuvp_ref, ckv_ref, kr_ref, *rest, pages_per_step, td):
    kv_refs = rest[:pages_per_step]
    krt_refs = rest[pages_per_step:2 * pages_per_step]
    o_ref = rest[2 * pages_per_step]
    qlat_ref, qrope_ref, m_ref, l_ref, acc_ref, kall_ref, krt_ref = rest[2 * pages_per_step + 1:]
    j = pl.program_id(1)
    rows = MLA_HEADS * td

    @pl.when(j == 0)
    def _():
        for hd in range(MLA_HEADS):
            qh = q_ref[hd].astype(jnp.float32)
            qlat_ref[hd * td:(hd + 1) * td, :] = _dot_nt(_bf(qh[:, :QK_NOPE]), wuk_ref[hd])
            qrope_ref[hd * td:(hd + 1) * td, :] = qh[:, QK_NOPE:QK_NOPE + QK_ROPE]
        m_ref[...] = jnp.full((rows, 1), NEG_INF, jnp.float32)
        l_ref[...] = jnp.zeros((rows, 1), jnp.float32)
        acc_ref[...] = jnp.zeros((rows, KV_LORA), jnp.float32)

    for r in range(pages_per_step):
        kall_ref[r * PAGE_SIZE:(r + 1) * PAGE_SIZE, :] = _bf(kv_refs[r][...])
        krt_ref[:, r * PAGE_SIZE:(r + 1) * PAGE_SIZE] = _bf(krt_refs[r][...])
    qlat = _bf(qlat_ref[...])
    qrope = _bf(qrope_ref[...])
    kall = kall_ref[...]
    s = _dot_nt(qlat, kall) + _dot(qrope, krt_ref[...])
    m = m_ref[...]
    m_new = jnp.maximum(m, jnp.max(s, axis=-1, keepdims=True))
    alpha = jnp.exp2(m - m_new)
    p = jnp.exp2(s - m_new)
    l = alpha * l_ref[...] + jnp.sum(p, axis=-1, keepdims=True)
    acc = alpha * acc_ref[...] + _dot(_bf(p), kall)
    m_ref[...], l_ref[...], acc_ref[...] = m_new, l, acc

    @pl.when(j == pl.num_programs(1) - 1)
    def _():
        pad = jnp.zeros((PAGE_SIZE - td, KV_LORA), jnp.float32)
        cnew = _bf(jnp.concatenate([ckv_ref[...], pad], axis=0))
        krnew = _bf(jnp.concatenate([kr_ref[...], pad[:, :QK_ROPE]], axis=0))
        sn = _dot_nt(qlat, cnew) + _dot_nt(qrope, krnew)
        row = lax.broadcasted_iota(jnp.int32, (rows, PAGE_SIZE), 0)
        col = lax.broadcasted_iota(jnp.int32, (rows, PAGE_SIZE), 1)
        sn = jnp.where(col <= row % td, sn, NEG_INF)
        m2 = jnp.maximum(m_new, jnp.max(sn, axis=-1, keepdims=True))
        a2 = jnp.exp2(m_new - m2)
        pn = jnp.exp2(sn - m2)
        l2 = a2 * l + jnp.sum(pn, axis=-1, keepdims=True)
        acc2 = a2 * acc + _dot(_bf(pn), cnew)
        olat = _bf(acc2 / l2)
        out = jnp.zeros((td, MLA_HEADS * V_HEAD), jnp.float32)
        for hd in range(MLA_HEADS):
            out = out + _dot(olat[hd * td:(hd + 1) * td, :], wuvp_ref[hd])
        o_ref[...] = out


def _attn_sample(page_table, q, wuk_h, wuv_placed, ckv, kr, cache_kv, cache_krt, layer, row0, dec_batch, td,
                 pages_per_step):
    n_pages = page_table.shape[1]
    steps = n_pages // pages_per_step
    blk0 = row0 // td
    rows = MLA_HEADS * td
    keys = pages_per_step * PAGE_SIZE

    def page_spec(shape, r):
        return pl.BlockSpec((None, None) + shape,
                            lambda b, j, pt, r=r: (layer, pt[b * n_pages + j * pages_per_step + r], 0, 0))

    in_specs = [pl.BlockSpec((MLA_HEADS, td, HEAD_SLOT), lambda b, j, pt: (0, blk0 + b, 0)),
                pl.BlockSpec(wuk_h.shape, lambda b, j, pt: (0, 0, 0)),
                pl.BlockSpec(wuv_placed.shape, lambda b, j, pt: (0, 0, 0)),
                pl.BlockSpec((td, KV_LORA), lambda b, j, pt: (blk0 + b, 0)),
                pl.BlockSpec((td, QK_ROPE), lambda b, j, pt: (blk0 + b, 0))]
    in_specs += [page_spec((PAGE_SIZE, KV_LORA), r) for r in range(pages_per_step)]
    in_specs += [page_spec((QK_ROPE, PAGE_SIZE), r) for r in range(pages_per_step)]
    grid_spec = pltpu.PrefetchScalarGridSpec(
        num_scalar_prefetch=1,
        grid=(dec_batch, steps),
        in_specs=in_specs,
        out_specs=pl.BlockSpec((td, MLA_HEADS * V_HEAD), lambda b, j, pt: (b, 0)),
        scratch_shapes=[pltpu.VMEM((rows, KV_LORA), jnp.float32), pltpu.VMEM((rows, QK_ROPE), jnp.float32),
                        pltpu.VMEM((rows, 1), jnp.float32), pltpu.VMEM((rows, 1), jnp.float32),
                        pltpu.VMEM((rows, KV_LORA), jnp.float32),
                        pltpu.VMEM((keys, KV_LORA), jnp.bfloat16), pltpu.VMEM((QK_ROPE, keys), jnp.bfloat16)])
    return pl.pallas_call(
        functools.partial(_attn_sample_kernel, pages_per_step=pages_per_step, td=td),
        grid_spec=grid_spec,
        out_shape=jax.ShapeDtypeStruct((dec_batch * td, MLA_HEADS * V_HEAD), jnp.float32),
        compiler_params=pltpu.CompilerParams(dimension_semantics=("parallel", "arbitrary"),
                                             vmem_limit_bytes=VMEM_LIMIT),
        name="attn_sample",
    )(page_table.reshape(-1), q, wuk_h, wuv_placed, ckv, kr, *([cache_kv] * pages_per_step),
      *([cache_krt] * pages_per_step))


def _place_wuv(w_uv):
    eye = jnp.eye(MLA_HEADS, dtype=w_uv.dtype)
    placed = jnp.einsum('chv,hg->hcgv', w_uv, eye)
    return _bf(placed.reshape(MLA_HEADS, KV_LORA, MLA_HEADS * V_HEAD))


def _ret_gammas():
    return [1.0 - 2.0 ** (-5.0 - h) for h in range(RET_HEADS)]


def _ret_kernel(*refs, nb, parts, decay_len):
    q_refs, k_refs, v_refs, g_refs = (refs[g * parts:(g + 1) * parts] for g in range(4))
    s0_ref, dmask_ref, qdec_ref, kdec_ref, gn_ref, o_ref, sfin_ref, s_ref = refs[4 * parts:]
    c = pl.program_id(1)

    @pl.when(c == 0)
    def _():
        s_ref[...] = s0_ref[...]

    qdec = qdec_ref[...]
    kdec = kdec_ref[...]
    gn = gn_ref[...]
    for sq in range(parts * nb):
        p, bi = divmod(sq, nb)
        q = q_refs[p][bi]
        k = k_refs[p][bi]
        v = v_refs[p][bi]
        kd = k * kdec
        gate = _silu(g_refs[p][bi])
        for h in range(RET_HEADS):
            sl = slice(h * RET_DK, (h + 1) * RET_DK)
            qh, kh, vh = _bf(q[:, sl]), _bf(k[:, sl]), _bf(v[:, sl])
            a = _dot_nt(qh, kh) * dmask_ref[h]
            s = s_ref[sq, h]
            o = _dot(_bf(a), vh) + _dot(qh, _bf(s)) * qdec[:, sl]
            s_ref[sq, h] = s * (_ret_gammas()[h] ** decay_len) + _dot_tn(_bf(kd[:, sl]), vh)
            oc = o - jnp.mean(o, axis=-1, keepdims=True)
            y = oc * lax.rsqrt(jnp.mean(oc * oc, axis=-1, keepdims=True) + EPS)
            o_ref[sq, :, sl] = _bf(y * gn[:, sl] * gate[:, sl])

    @pl.when(c == pl.num_programs(1) - 1)
    def _():
        sfin_ref[...] = s_ref[...]


def _ret_consts(chunk, decay_len):
    lg = jnp.log(jnp.asarray(_ret_gammas(), jnp.float32))
    idx = jnp.arange(chunk, dtype=jnp.float32)
    diff = idx[:, None] - idx[None, :]
    dmask = jnp.where(diff >= 0, jnp.exp(lg[:, None, None] * jnp.maximum(diff, 0.0)), 0.0)
    qdec = jnp.exp(lg[None, :] * (idx[:, None] + 1.0))
    kdec = jnp.exp(lg[None, :] * (decay_len - 1.0 - idx[:, None]))
    rep = lambda a: jnp.repeat(a, RET_DK, axis=1)
    return dmask, rep(qdec), rep(kdec)


def _rec_specs(nb, parts, chunk, nch, groups):
    return [pl.BlockSpec((nb, chunk, REC_W), lambda b, c, g=g, p=p: ((b * parts + p) * nch + c, 0, g))
            for g in groups for p in range(parts)]


def _rec_call(kernel_fn, name, groups, zp3, s0, extra, nb, parts, nch):
    chunk = zp3.shape[1]
    seqs = s0.shape[0]
    per_step = parts * nb
    full = lambda a: pl.BlockSpec(a.shape, lambda b, c: (0,) * a.ndim)
    st = pl.BlockSpec((per_step,) + s0.shape[1:], lambda b, c: (b, 0, 0, 0))
    return pl.pallas_call(
        kernel_fn,
        grid=(seqs // per_step, nch),
        in_specs=_rec_specs(nb, parts, chunk, nch, groups) + [st] + [full(a) for a in extra],
        out_specs=(pl.BlockSpec((per_step, None, chunk, REC_W), lambda b, c: (b, c, 0, 0)), st),
        out_shape=(jax.ShapeDtypeStruct((seqs, nch, chunk, REC_W), jnp.bfloat16),
                   jax.ShapeDtypeStruct(s0.shape, jnp.float32)),
        scratch_shapes=[pltpu.VMEM((per_step,) + s0.shape[1:], jnp.float32)],
        compiler_params=pltpu.CompilerParams(dimension_semantics=("parallel", "arbitrary"),
                                             vmem_limit_bytes=VMEM_LIMIT),
        name=name,
    )(*([zp3] * (len(groups) * parts)), s0, *extra)


def _retention(zp3, s0, gn, nb, parts, nch, decay_len):
    consts = _ret_consts(zp3.shape[1], decay_len)
    return _rec_call(functools.partial(_ret_kernel, nb=nb, parts=parts, decay_len=decay_len), "retention",
                     (ZP_RQ, ZP_RK, ZP_RV, ZP_RG), zp3, s0, consts + (gn,), nb, parts, nch)


def _hgrn_sum_matrix(chunk):
    import numpy as np
    t = np.arange(chunk)[:, None]
    r = np.arange(chunk)[None, :]
    blocks = [r <= t, r > t]
    m = 2
    while m <= chunk:
        half = m // 2
        mid = (t // m) * m + half
        upper = (t % m) >= half
        blocks.append(upper & (r >= mid) & (r <= t))
        blocks.append(~upper & (r > t) & (r <= mid - 1))
        m *= 2
    return jnp.asarray(np.concatenate(blocks, axis=0).astype(np.float32), jnp.bfloat16)


def _hgrn_kernel(*refs, nb, parts, chunk):
    q_refs, f_refs, k_refs, v_refs, g_refs = (refs[g * parts:(g + 1) * parts] for g in range(5))
    s0_ref, msum_ref, gn_ref, o_ref, sfin_ref, st_ref = refs[5 * parts:]
    c = pl.program_id(1)
    nlev = chunk.bit_length() - 1

    @pl.when(c == 0)
    def _():
        for bi in range(parts * nb):
            for h in range(HG_HEADS):
                st_ref[bi, h] = s0_ref[bi, h].T

    gn = gn_ref[...]
    rowi = lax.broadcasted_iota(jnp.int32, (chunk, REC_W), 0)
    r2 = lax.broadcasted_iota(jnp.int32, (chunk, chunk), 0)
    c2 = lax.broadcasted_iota(jnp.int32, (chunk, chunk), 1)
    for sq in range(parts * nb):
        p, bi = divmod(sq, nb)
        q = q_refs[p][bi]
        k = k_refs[p][bi]
        v = v_refs[p][bi]
        gate = _silu(g_refs[p][bi])
        f1, f2, f3 = _split3(f_refs[p][bi])
        sums = _dot(msum_ref[...], jnp.concatenate([f1, f2, f3], axis=1))
        sums = sums[:, :REC_W] + sums[:, REC_W:2 * REC_W] + sums[:, 2 * REC_W:]
        part = lambda i: sums[i * chunk:(i + 1) * chunk]
        b = part(0)
        qb = _bf(q * jnp.exp(b))
        ks = _bf(k * jnp.exp(part(1)))
        e_last = jnp.exp(b[chunk - 1:chunk, :])
        vb = _bf(v)
        qs, kls = [_bf(q)], [_bf(k)]
        for lev in range(1, nlev + 1):
            upper = (rowi & (1 << (lev - 1))) != 0
            qs.append(_bf(jnp.where(upper, q * jnp.exp(part(2 * lev)), 0.0)))
            kls.append(_bf(jnp.where(upper, 0.0, k * jnp.exp(part(2 * lev + 1)))))
        for h in range(HG_HEADS):
            sl = slice(h * HG_DK, (h + 1) * HG_DK)
            a = jnp.where(r2 == c2, _dot_nt(qs[0][:, sl], kls[0][:, sl]), 0.0)
            for lev in range(1, nlev + 1):
                same = (r2 >> lev) == (c2 >> lev)
                a = a + jnp.where(same, _dot_nt(qs[lev][:, sl], kls[lev][:, sl]), 0.0)
            st = st_ref[sq, h]
            o = _dot(_bf(a), vb[:, sl]) + _dot_nt(qb[:, sl], _bf(st))
            st_ref[sq, h] = st * e_last[:, sl] + _dot_tn(vb[:, sl], ks[:, sl])
            y = o * lax.rsqrt(jnp.mean(o * o, axis=-1, keepdims=True) + EPS)
            o_ref[sq, :, sl] = _bf(y * gn[:, sl] * gate[:, sl])

    @pl.when(c == pl.num_programs(1) - 1)
    def _():
        for bi in range(parts * nb):
            for h in range(HG_HEADS):
                sfin_ref[bi, h] = st_ref[bi, h].T


def _hgrn(zp3, s0, gn, nb, parts, nch):
    chunk = zp3.shape[1]
    return _rec_call(functools.partial(_hgrn_kernel, nb=nb, parts=parts, chunk=chunk), "hgrn2",
                     (ZP_GQ, ZP_GF, ZP_GK, ZP_GI, ZP_GG), zp3, s0, (_hgrn_sum_matrix(chunk), gn), nb, parts, nch)


def _dot3(ah, al, bh, bl, dot):
    return dot(ah, bh) + dot(al, bh) + dot(ah, bl)


def _split2(x):
    hi = _bf(x)
    return hi, _bf(x - hi.astype(jnp.float32))


def _out_proj_kernel(x_ref, omla_ref, oret_ref, ohg_ref, wout_ref, gffn_ref, wqh_ref, wql_ref, kh_ref, kl_ref,
                     x1_ref, h2t_ref, st_ref):
    mix = jnp.concatenate([omla_ref[...], oret_ref[...], ohg_ref[...]], axis=1)
    x1 = x_ref[...] + _dot(mix, wout_ref[...])
    x1_ref[...] = x1
    h2 = _rms(x1, gffn_ref[...])
    h2t_ref[...] = _bf(h2.T)
    hh, hl = _split2(h2)
    q = _dot3(hh, hl, wqh_ref[...], wql_ref[...], _dot)
    for p in range(2 * PEER_HEADS):
        qh, ql = _split2(q[:, p * PEER_HALF:(p + 1) * PEER_HALF])
        sp = _dot3(kh_ref[p], kl_ref[p], qh, ql, _dot_nt)
        for c in range(sp.shape[1] // LANES):
            st_ref[p, c] = sp[:, c * LANES:(c + 1) * LANES]


def _out_proj(x, omla, oret, ohg, wout, gffn, wqh, wql, kh, kl, tm):
    n = x.shape[0]
    full = lambda a: pl.BlockSpec(a.shape, lambda i: (0,) * a.ndim)
    row = lambda w: pl.BlockSpec((tm, w), lambda i: (i, 0))
    return pl.pallas_call(
        _out_proj_kernel,
        grid=(n // tm,),
        in_specs=[row(D_MODEL), row(MLA_HEADS * V_HEAD), row(REC_W), row(REC_W), full(wout), full(gffn), full(wqh),
                  full(wql), full(kh), full(kl)],
        out_specs=(row(D_MODEL), pl.BlockSpec((D_MODEL, tm), lambda i: (0, i)),
                   pl.BlockSpec((2 * PEER_HEADS, tm // LANES, PEER_NKEYS, LANES), lambda i: (0, i, 0, 0))),
        out_shape=(jax.ShapeDtypeStruct((n, D_MODEL), jnp.float32),
                   jax.ShapeDtypeStruct((D_MODEL, n), jnp.bfloat16),
                   jax.ShapeDtypeStruct((2 * PEER_HEADS, n // LANES, PEER_NKEYS, LANES), jnp.float32)),
        compiler_params=pltpu.CompilerParams(dimension_semantics=("parallel",), vmem_limit_bytes=VMEM_LIMIT),
        name="out_proj_peer_scores",
    )(x, omla, oret, ohg, wout, gffn, wqh, wql, kh, kl)


def _extract_top(x, count, on_value):
    rowf = lax.broadcasted_iota(jnp.int32, x.shape, 1).astype(jnp.float32)
    for r in range(count):
        m = jnp.max(x, axis=1, keepdims=True)
        first = jnp.min(jnp.where(x == m, rowf, float(x.shape[1])), axis=1, keepdims=True)
        hit = rowf == first
        on_value(r, m, hit)
        x = jnp.where(hit, NEG_INF, x)
    return x


SUB = 8


G_S2, G_E2, G_THETA, G_COEF = range(4)


def _peer_select_kernel(s_ref, gates_ref, v1_ref, v2_ref, cand_ref):
    s1 = s_ref[:, 0]
    s2 = s_ref[:, 1]

    def keep(ref):
        def on_value(r, m, hit):
            ref[:, r:r + 1, :] = m
        return on_value

    _extract_top(s1, PEER_TOPK + 1, keep(v1_ref))
    _extract_top(s2, PEER_TOPK + 1, keep(v2_ref))
    for r1 in range(SUB):
        cand_ref[:, r1 * SUB:(r1 + 1) * SUB, :] = v1_ref[:, r1:r1 + 1, :] + v2_ref[:, 0:SUB, :]
    cand_ref[:, SUB * SUB:SUB * SUB + SUB, :] = v1_ref[:, 0:1, :] + v2_ref[:, SUB:2 * SUB, :]
    cand_ref[:, SUB * SUB + SUB:SUB * SUB + 2 * SUB, :] = v1_ref[:, SUB:2 * SUB, :] + v2_ref[:, 0:1, :]
    best = v1_ref[:, 0:1, :] + v2_ref[:, 0:1, :]
    picked = []
    rest = _extract_top(cand_ref[...], PEER_TOPK, lambda r, m, hit: picked.append(m))
    z = jnp.zeros_like(best)
    for m in picked:
        z = z + jnp.exp(m - best)
    nxt = jnp.max(rest, axis=1, keepdims=True)
    nxt = jnp.maximum(nxt, v1_ref[:, PEER_TOPK:PEER_TOPK + 1, :] + v2_ref[:, 0:1, :])
    nxt = jnp.maximum(nxt, v1_ref[:, 0:1, :] + v2_ref[:, PEER_TOPK:PEER_TOPK + 1, :])
    tau = 0.5 * (picked[-1] + nxt)
    gates_ref[:, G_S2] = s2
    gates_ref[:, G_E2] = jnp.exp(s2 - v2_ref[:, 0:1, :])
    gates_ref[:, G_THETA] = tau - s1
    gates_ref[:, G_COEF] = jnp.exp(s1 - v1_ref[:, 0:1, :]) * (1.0 / z)


def _peer_select(st):
    nt = st.shape[1]
    s5 = st.reshape(PEER_HEADS, 2, nt, PEER_NKEYS, LANES)
    vrows = SUB * ((PEER_TOPK + 1 + SUB - 1) // SUB)
    return pl.pallas_call(
        _peer_select_kernel,
        grid=(nt,),
        in_specs=[pl.BlockSpec((PEER_HEADS, 2, None, PEER_NKEYS, LANES), lambda t: (0, 0, t, 0, 0))],
        out_specs=pl.BlockSpec((PEER_HEADS, None, 4, PEER_NKEYS, LANES), lambda t: (0, t, 0, 0, 0)),
        out_shape=jax.ShapeDtypeStruct((PEER_HEADS, nt, 4, PEER_NKEYS, LANES), jnp.float32),
        scratch_shapes=[pltpu.VMEM((PEER_HEADS, vrows, LANES), jnp.float32),
                        pltpu.VMEM((PEER_HEADS, vrows, LANES), jnp.float32),
                        pltpu.VMEM((PEER_HEADS, SUB * SUB + 2 * SUB, LANES), jnp.float32)],
        compiler_params=pltpu.CompilerParams(dimension_semantics=("parallel",), vmem_limit_bytes=VMEM_LIMIT),
        name="peer_select",
    )(s5)


def _peer_expert_kernel(h2t_ref, u_ref, vt_ref, gates_ref, x1_ref, gfin_ref, o_ref, acc_ref, p_ref, at_ref, *,
                        ni, final_norm):
    step = pl.program_id(1)

    @pl.when(step == 0)
    def _():
        acc_ref[...] = jnp.zeros_like(acc_ref)

    tm = h2t_ref.shape[1]
    strips = tm // LANES
    at_ref[...] = _dot(u_ref[...], h2t_ref[...])

    def across(h, plane, rows):
        return jnp.concatenate([gates_ref[h, c, plane, rows, :] for c in range(strips)], axis=1)

    every = slice(None)
    for ib in range(ni):
        row_i = pl.ds(step * ni + ib, 1)
        w = None
        for h in range(PEER_HEADS):
            sel = across(h, G_S2, every) >= across(h, G_THETA, row_i)
            term = jnp.where(sel, across(h, G_E2, every) * across(h, G_COEF, row_i), 0.0)
            w = term if w is None else w + term
        rows = slice(ib * PEER_NKEYS, (ib + 1) * PEER_NKEYS)
        a = at_ref[rows, :]
        gelu = 0.5 * a * (1.0 + lax.erf(a * (2.0 ** -0.5)))
        p_ref[rows, :] = _bf(gelu * w)
    acc_ref[...] += _dot(vt_ref[...], p_ref[...])

    @pl.when(step == pl.num_programs(1) - 1)
    def _():
        x2 = x1_ref[...] + acc_ref[...].T
        o_ref[...] = _rms(x2, gfin_ref[...]) if final_norm else x2


def _peer_experts(h2t, u, vt, gates, x1, gfin, tm, ni, final_norm):
    n = x1.shape[0]
    et = ni * PEER_NKEYS
    blocks = PEER_NKEYS // ni
    return pl.pallas_call(
        functools.partial(_peer_expert_kernel, ni=ni, final_norm=final_norm),
        grid=(n // tm, blocks),
        in_specs=[pl.BlockSpec((D_MODEL, tm), lambda t, i: (0, t)),
                  pl.BlockSpec((et, D_MODEL), lambda t, i: (i, 0)),
                  pl.BlockSpec((None, D_MODEL, et), lambda t, i: (i, 0, 0)),
                  pl.BlockSpec((PEER_HEADS, tm // LANES, 4, PEER_NKEYS, LANES), lambda t, i: (0, t, 0, 0, 0)),
                  pl.BlockSpec((tm, D_MODEL), lambda t, i: (t, 0)),
                  pl.BlockSpec(gfin.shape, lambda t, i: (0, 0))],
        out_specs=pl.BlockSpec((tm, D_MODEL), lambda t, i: (t, 0)),
        out_shape=jax.ShapeDtypeStruct((n, D_MODEL), jnp.float32),
        scratch_shapes=[pltpu.VMEM((D_MODEL, tm), jnp.float32), pltpu.VMEM((et, tm), jnp.bfloat16),
                        pltpu.VMEM((et, tm), jnp.float32)],
        compiler_params=pltpu.CompilerParams(dimension_semantics=("parallel", "arbitrary"),
                                             vmem_limit_bytes=VMEM_LIMIT),
        name="peer_experts",
    )(h2t, u, vt, gates, x1, gfin)


TOKEN_TILE = 512
ATTN_BLOCK = 1024
REC_CHUNK = 128
SAMPLE_CHUNK = 16
SAMPLE_SEQS_PER_STEP = 8
PAGES_PER_STEP = 64
EXPERT_TOKEN_TILE = 512
EXPERT_ROWS_PER_STEP = 8


def kernel(x_prompt, x_sample, cache_kv_latent, cache_k_rope, state_retention, state_hgrn, page_table,
           w_in, q_norm, w_uq, kv_norm, w_uk, w_uv, ret_norm, hg_norm, hg_lower_bounds, w_out,
           attn_norm, ffn_norm, final_norm, peer_wq, peer_keys, peer_u, peer_v):
    batch, seq, d = x_prompt.shape
    dec_batch, td, _ = x_sample.shape
    depth = w_in.shape[0]
    n_p, n_s = batch * seq, dec_batch * td
    past_len = page_table.shape[1] * PAGE_SIZE
    f32 = jnp.float32

    x = jnp.concatenate([x_prompt.reshape(n_p, d), x_sample.reshape(n_s, d)], axis=0)
    pos = jnp.concatenate([jnp.tile(jnp.arange(seq, dtype=jnp.int32), batch),
                           jnp.tile(past_len + jnp.arange(td, dtype=jnp.int32), dec_batch)])
    tabs = _rope_tables(pos)
    cache_krt = jnp.swapaxes(cache_k_rope, 2, 3)
    pages_per_step = math.gcd(PAGES_PER_STEP, page_table.shape[1])
    lb_soft = jax.nn.softmax(hg_lower_bounds.astype(f32), axis=0)
    lbs = jnp.cumsum(lb_soft, axis=0) - lb_soft[0]
    zeros_state = jnp.zeros((batch, RET_HEADS, RET_DK, RET_DV), f32)
    nch = seq // REC_CHUNK
    row2 = lambda a: a.reshape(1, -1)

    outs = {k: [] for k in ("p_ckv", "p_kr", "p_ret", "p_hg", "s_ckv", "s_kr", "s_ret", "s_hg")}
    for l in range(depth):
        wbig, wuq, wuk, wuv = _prep_layer_weights(w_in[l], w_uq[l], w_uk[l], w_uv[l])
        ckv, kr, q, k, v, zp = _proj_in(x, row2(attn_norm[l]), wbig, row2(q_norm[l]), row2(kv_norm[l]), wuq, wuk,
                                        wuv, row2(lbs[l]), tabs, TOKEN_TILE)
        o_p = _attn_prompt(q, k, v, batch, seq, ATTN_BLOCK)
        o_s = _attn_sample(page_table, q, _bf(jnp.transpose(w_uk[l], (1, 0, 2))), _place_wuv(w_uv[l]), ckv, kr,
                           cache_kv_latent, cache_krt, l, n_p, dec_batch, td, pages_per_step)
        o_mla = jnp.concatenate([o_p, _bf(o_s)], axis=0)
        zp_p = zp.reshape(-1, REC_CHUNK, zp.shape[-1])
        zp_s = jnp.pad(zp[n_p:].reshape(dec_batch, td, -1), ((0, 0), (0, SAMPLE_CHUNK - td), (0, 0)))
        gn_ret, gn_hg = row2(ret_norm[l]), row2(hg_norm[l])
        ret_p, ret_sp = _retention(zp_p, zeros_state, gn_ret, 1, batch, nch, REC_CHUNK)
        ret_s, ret_ss = _retention(zp_s, state_retention[l], gn_ret, SAMPLE_SEQS_PER_STEP, 1, 1, td)
        hg_p, hg_sp = _hgrn(zp_p, zeros_state, gn_hg, 1, batch, nch)
        hg_s, hg_ss = _hgrn(zp_s, state_hgrn[l], gn_hg, SAMPLE_SEQS_PER_STEP, 1, 1)
        join = lambda p, s: jnp.concatenate([p.reshape(n_p, REC_W), s[:, 0, :td].reshape(n_s, REC_W)], axis=0)
        o_ret, o_hg = join(ret_p, ret_s), join(hg_p, hg_s)
        wqh, wql = _split2(peer_wq[l])
        kh, kl = _split2(peer_keys[l].reshape(2 * PEER_HEADS, PEER_NKEYS, PEER_HALF))
        x1, h2t, st = _out_proj(x, o_mla, o_ret, o_hg, _bf(w_out[l]), row2(ffn_norm[l]), wqh, wql, kh, kl,
                                TOKEN_TILE)
        gates = _peer_select(st)
        et = EXPERT_ROWS_PER_STEP * PEER_NKEYS
        vt = _bf(jnp.transpose(peer_v[l].reshape(-1, et, d), (0, 2, 1)))
        x = _peer_experts(h2t, _bf(peer_u[l]), vt, gates, x1, row2(final_norm), EXPERT_TOKEN_TILE,
                          EXPERT_ROWS_PER_STEP, l == depth - 1)
        outs["p_ckv"].append(ckv[:n_p].reshape(batch, seq, KV_LORA))
        outs["p_kr"].append(kr[:n_p].reshape(batch, seq, QK_ROPE))
        outs["p_ret"].append(ret_sp)
        outs["p_hg"].append(hg_sp)
        outs["s_ckv"].append(ckv[n_p:].reshape(dec_batch, td, KV_LORA))
        outs["s_kr"].append(kr[n_p:].reshape(dec_batch, td, QK_ROPE))
        outs["s_ret"].append(ret_ss)
        outs["s_hg"].append(hg_ss)
    y_prompt = x[:n_p].reshape(batch, seq, d)
    y_sample = x[n_p:].reshape(dec_batch, td, d)
    return (y_prompt, y_sample) + tuple(jnp.stack(outs[k]) for k in
                                        ("p_ckv", "p_kr", "p_ret", "p_hg", "s_ckv", "s_kr", "s_ret", "s_hg"))
```
